```python
import math
import jax, jax.numpy as jnp
from jax import lax
import numpy as np

D_MODEL = 1024
BATCH = 8
SEQ = 2048
DEPTH = 2

N_EVEN = (DEPTH + 1) // 2
N_ODD = DEPTH // 2
ALPHA = (2 * DEPTH) ** 0.25
BETA = (8 * DEPTH) ** -0.25
LN_EPS = 1e-5

SC_WIDTH = D_MODEL
SC_KERNEL = 3

SSM_INNER = D_MODEL
SSM_HEADDIM = 64
SSM_HEADS = SSM_INNER // SSM_HEADDIM
SSM_GROUPS = 2
SSM_STATE = 128
SSM_CONV = 4
SSM_CHUNK = 128
SSM_XBC = SSM_INNER + 2 * SSM_GROUPS * SSM_STATE
MIX0_IN = 3 * SC_WIDTH + SSM_INNER + SSM_XBC + SSM_HEADS
MIX0_OUT = SC_WIDTH + SSM_INNER

FOX_HEADS = 16
FOX_HEADDIM = D_MODEL // FOX_HEADS
FOX_WIDTH = FOX_HEADS * FOX_HEADDIM
FOX_IN = 3 * FOX_WIDTH + FOX_HEADS
FOX_BLOCK = 128

D_FF = 2816
FFN_KERNEL = 3

kernel_name = 'hybrid_sconv_ssd_fox_deepnorm'


def _split(h, sizes):
    idx = np.cumsum(sizes)[:-1].tolist()
    return jnp.split(h, idx, axis=-1)


def causal_dwconv(x, w, b=None):
    k, c = w.shape
    y = lax.conv_general_dilated(x, w[:, None, :].astype(x.dtype), window_strides=(1,), padding=[(k - 1, 0)], dimension_numbers=('NWC', 'WIO', 'NWC'), feature_group_count=c)
    return y if b is None else y + b.astype(y.dtype)


def layer_norm(x, g, b):
    xf = x.astype(jnp.float32)
    mu = jnp.mean(xf, -1, keepdims=True)
    var = jnp.mean(jnp.square(xf - mu), -1, keepdims=True)
    return ((xf - mu) * lax.rsqrt(var + LN_EPS) * g + b).astype(x.dtype)


def gated_rmsnorm(y, z, g):
    bsz, l, _ = y.shape
    u = (y.astype(jnp.float32) * jax.nn.silu(z.astype(jnp.float32))).reshape(bsz, l, SSM_GROUPS, -1)
    u = u * lax.rsqrt(jnp.mean(jnp.square(u), -1, keepdims=True) + LN_EPS)
    return (u.reshape(bsz, l, SSM_INNER) * g).astype(y.dtype)


def ssd_chunked(xs, dt, a, bm, cm):
    bsz, l = xs.shape[:2]
    nc, q, r = l // SSM_CHUNK, SSM_CHUNK, SSM_HEADS // SSM_GROUPS
    x = xs.astype(jnp.float32).reshape(bsz, nc, q, SSM_GROUPS, r, SSM_HEADDIM)
    dtc = dt.reshape(bsz, nc, q, SSM_GROUPS, r)
    xdt = x * dtc[..., None]
    bc = bm.astype(jnp.float32).reshape(bsz, nc, q, SSM_GROUPS, SSM_STATE)
    cc = cm.astype(jnp.float32).reshape(bsz, nc, q, SSM_GROUPS, SSM_STATE)
    a_cs = jnp.cumsum((dtc * a.reshape(SSM_GROUPS, r)).transpose(0, 1, 3, 4, 2), axis=-1)
    causal = jnp.tril(jnp.ones((q, q), dtype=bool))
    seg = jnp.exp(jnp.where(causal, a_cs[..., :, None] - a_cs[..., None, :], -jnp.inf))
    cb = jnp.einsum('bctgn,bcsgn->bcgts', cc, bc)
    y_diag = jnp.einsum('bcgts,bcgrts,bcsgrp->bctgrp', cb, seg, xdt)
    decay_to_end = jnp.exp(a_cs[..., -1:] - a_cs)
    chunk_states = jnp.einsum('bcsgn,bcgrs,bcsgrp->bcgrpn', bc, decay_to_end, xdt)
    chunk_decay = jnp.exp(a_cs[..., -1])

    def step(h, inp):
        st, dec = inp
        return h * dec[..., None, None] + st, h

    h0 = jnp.zeros((bsz, SSM_GROUPS, r, SSM_HEADDIM, SSM_STATE), jnp.float32)
    _, prev = lax.scan(step, h0, (jnp.moveaxis(chunk_states, 1, 0), jnp.moveaxis(chunk_decay, 1, 0)))
    prev = jnp.moveaxis(prev, 0, 1)
    y_off = jnp.einsum('bctgn,bcgrpn,bcgrt->bctgrp', cc, prev, jnp.exp(a_cs))
    return (y_diag + y_off).reshape(bsz, l, SSM_HEADS, SSM_HEADDIM)


def sconv_ssd_mixer(x, w_in, sc_conv_w, ssm_conv_w, ssm_conv_b, dt_bias, a_log, d_skip, norm_g, w_out):
    bsz, l, _ = x.shape
    h = x @ w_in
    sc_b, sc_c, sc_h, z, xbc, dt_raw = _split(h, [SC_WIDTH, SC_WIDTH, SC_WIDTH, SSM_INNER, SSM_XBC, SSM_HEADS])
    y_a = sc_b * causal_dwconv(sc_c * sc_h, sc_conv_w)
    xbc = jax.nn.silu(causal_dwconv(xbc, ssm_conv_w, ssm_conv_b))
    xs, bm, cm = _split(xbc, [SSM_INNER, SSM_GROUPS * SSM_STATE, SSM_GROUPS * SSM_STATE])
    xs = xs.reshape(bsz, l, SSM_HEADS, SSM_HEADDIM)
    dt = jax.nn.softplus(dt_raw.astype(jnp.float32) + dt_bias.astype(jnp.float32))
    a = -jnp.exp(a_log.astype(jnp.float32))
    y = ssd_chunked(xs, dt, a, bm.reshape(bsz, l, SSM_GROUPS, SSM_STATE), cm.reshape(bsz, l, SSM_GROUPS, SSM_STATE))
    y = y + d_skip.astype(jnp.float32)[:, None] * xs.astype(jnp.float32)
    y_b = gated_rmsnorm(y.reshape(bsz, l, SSM_INNER).astype(x.dtype), z, norm_g)
    return jnp.concatenate([y_a, y_b], axis=-1) @ w_out


def fox_mixer(x, w_in, b_f, w_out):
    bsz, l, _ = x.shape
    nb = l // FOX_BLOCK
    h = x @ w_in
    q, k, v, f_logit = _split(h, [FOX_WIDTH, FOX_WIDTH, FOX_WIDTH, FOX_HEADS])
    heads = lambda t: t.reshape(bsz, l, FOX_HEADS, FOX_HEADDIM).transpose(0, 2, 1, 3)
    q, k, v = heads(q), heads(k), heads(v)
    log_f = jax.nn.log_sigmoid(f_logit.astype(jnp.float32) + b_f.astype(jnp.float32))
    cum = jnp.cumsum(log_f, axis=1).transpose(0, 2, 1)
    scale = FOX_HEADDIM ** -0.5
    key_pos = jnp.arange(l)

    def block(args):
        q_blk, cum_blk, i = args
        qpos = i * FOX_BLOCK + jnp.arange(FOX_BLOCK)
        s = jnp.einsum('bhqd,bhkd->bhqk', q_blk, k).astype(jnp.float32) * scale + cum_blk[..., None] - cum[:, :, None, :]
        s = jnp.where(key_pos[None, :] <= qpos[:, None], s, -jnp.inf)
        p = jax.nn.softmax(s, axis=-1)
        return jnp.einsum('bhqk,bhkd->bhqd', p.astype(v.dtype), v)

    q_blocks = q.reshape(bsz, FOX_HEADS, nb, FOX_BLOCK, FOX_HEADDIM).transpose(2, 0, 1, 3, 4)
    cum_blocks = cum.reshape(bsz, FOX_HEADS, nb, FOX_BLOCK).transpose(2, 0, 1, 3)
    o = lax.map(block, (q_blocks, cum_blocks, jnp.arange(nb)))
    o = o.transpose(1, 0, 3, 2, 4).reshape(bsz, l, FOX_WIDTH)
    return o @ w_out


def conv_ffn(x, w_up, conv_w, conv_b, w_down):
    h = causal_dwconv(x @ w_up, conv_w, conv_b)
    u, g = _split(h, [D_FF, D_FF])
    return (u * jax.nn.silu(g)) @ w_down


def setup_inputs(seed: int = 0) -> dict:
    key = jax.random.key(seed)
    ks = jax.random.split(key, 21)
    f32 = jnp.float32

    def nrm(k, shape, scale):
        return jax.random.normal(k, shape, f32) * scale

    x = nrm(ks[0], (BATCH, SEQ, D_MODEL), 1.0)
    sc_ssm_w_in = nrm(ks[1], (N_EVEN, D_MODEL, MIX0_IN), D_MODEL ** -0.5)
    sc_conv_w = nrm(ks[2], (N_EVEN, SC_KERNEL, SC_WIDTH), SC_KERNEL ** -0.5)
    ssm_conv_w = nrm(ks[3], (N_EVEN, SSM_CONV, SSM_XBC), SSM_CONV ** -0.5)
    ssm_conv_b = nrm(ks[4], (N_EVEN, SSM_XBC), 0.02)
    dt0 = jnp.exp(jax.random.uniform(ks[5], (N_EVEN, SSM_HEADS), f32, minval=math.log(1e-3), maxval=math.log(1e-1)))
    ssm_dt_bias = dt0 + jnp.log(-jnp.expm1(-dt0))
    ssm_a_log = jnp.log(jax.random.uniform(ks[6], (N_EVEN, SSM_HEADS), f32, minval=1.0, maxval=16.0))
    ssm_d = 1.0 + nrm(ks[7], (N_EVEN, SSM_HEADS), 0.1)
    ssm_norm_g = 1.0 + nrm(ks[8], (N_EVEN, SSM_INNER), 0.02)
    sc_ssm_w_out = nrm(ks[9], (N_EVEN, MIX0_OUT, D_MODEL), BETA * MIX0_OUT ** -0.5)
    col_scale = jnp.concatenate([jnp.ones((2 * FOX_WIDTH,), f32), jnp.full((FOX_WIDTH,), BETA, f32), jnp.ones((FOX_HEADS,), f32)])
    fox_w_in = nrm(ks[10], (N_ODD, D_MODEL, FOX_IN), D_MODEL ** -0.5) * col_scale
    fox_b_f = jax.random.uniform(ks[11], (N_ODD, FOX_HEADS), f32, minval=1.0, maxval=4.0)
    fox_w_out = nrm(ks[12], (N_ODD, FOX_WIDTH, D_MODEL), BETA * FOX_WIDTH ** -0.5)
    ffn_w_up = nrm(ks[13], (DEPTH, D_MODEL, 2 * D_FF), BETA * D_MODEL ** -0.5)
    ffn_conv_w = nrm(ks[14], (DEPTH, FFN_KERNEL, 2 * D_FF), FFN_KERNEL ** -0.5)
    ffn_conv_b = nrm(ks[15], (DEPTH, 2 * D_FF), 0.02)
    ffn_w_down = nrm(ks[16], (DEPTH, D_FF, D_MODEL), BETA * D_FF ** -0.5)
    ln_mix_g = 1.0 + nrm(ks[17], (DEPTH, D_MODEL), 0.02)
    ln_mix_b = nrm(ks[18], (DEPTH, D_MODEL), 0.02)
    ln_ffn_g = 1.0 + nrm(ks[19], (DEPTH, D_MODEL), 0.02)
    ln_ffn_b = nrm(ks[20], (DEPTH, D_MODEL), 0.02)
    return {'x': x, 'sc_ssm_w_in': sc_ssm_w_in, 'sc_conv_w': sc_conv_w, 'ssm_conv_w': ssm_conv_w, 'ssm_conv_b': ssm_conv_b, 'ssm_dt_bias': ssm_dt_bias, 'ssm_a_log': ssm_a_log, 'ssm_d': ssm_d, 'ssm_norm_g': ssm_norm_g, 'sc_ssm_w_out': sc_ssm_w_out, 'fox_w_in': fox_w_in, 'fox_b_f': fox_b_f, 'fox_w_out': fox_w_out, 'ffn_w_up': ffn_w_up, 'ffn_conv_w': ffn_conv_w, 'ffn_conv_b': ffn_conv_b, 'ffn_w_down': ffn_w_down, 'ln_mix_g': ln_mix_g, 'ln_mix_b': ln_mix_b, 'ln_ffn_g': ln_ffn_g, 'ln_ffn_b': ln_ffn_b}


def reference(x, sc_ssm_w_in, sc_conv_w, ssm_conv_w, ssm_conv_b, ssm_dt_bias, ssm_a_log, ssm_d, ssm_norm_g, sc_ssm_w_out, fox_w_in, fox_b_f, fox_w_out, ffn_w_up, ffn_conv_w, ffn_conv_b, ffn_w_down, ln_mix_g, ln_mix_b, ln_ffn_g, ln_ffn_b):
    for i in range(DEPTH):
        j = i // 2
        if i % 2 == 0:
            mix = sconv_ssd_mixer(x, sc_ssm_w_in[j], sc_conv_w[j], ssm_conv_w[j], ssm_conv_b[j], ssm_dt_bias[j], ssm_a_log[j], ssm_d[j], ssm_norm_g[j], sc_ssm_w_out[j])
        else:
            mix = fox_mixer(x, fox_w_in[j], fox_b_f[j], fox_w_out[j])
        x = layer_norm(ALPHA * x + mix, ln_mix_g[i], ln_mix_b[i])
        x = layer_norm(ALPHA * x + conv_ffn(x, ffn_w_up[i], ffn_conv_w[i], ffn_conv_b[i], ffn_w_down[i]), ln_ffn_g[i], ln_ffn_b[i])
    return x
```

```python
import functools

import jax
import jax.numpy as jnp
from jax import lax
from jax.experimental import pallas as pl
from jax.experimental.pallas import tpu as pltpu

F32 = jnp.float32
BF16 = jnp.bfloat16

DEPTH = 2
ALPHA = (2 * DEPTH) ** 0.25
LN_EPS = 1e-5

SSM_HEADDIM = 64
SSM_HEADS = 16
SSM_GROUPS = 2
SSM_STATE = 128
SSM_CHUNK = 128

FOX_HEADS = 16
FOX_HEADDIM = 64
FOX_EXT = 8

LANES = 128
CONV_HALO = 8
NEG_BIG = -1e30
VMEM_LIMIT = 56 * 1024 * 1024


def _params(sem):
    return pltpu.CompilerParams(dimension_semantics=sem, vmem_limit_bytes=VMEM_LIMIT)


def _mm(a, b):
    return jnp.dot(a, b, preferred_element_type=F32)


def _sigmoid(x):
    return 1.0 / (1.0 + jnp.exp(-x))


def _softplus(x):
    return jnp.maximum(x, 0.0) + jnp.log1p(jnp.exp(-jnp.abs(x)))


def _cumsum_rows(x):
    n = x.shape[0]
    row = lax.broadcasted_iota(jnp.int32, x.shape, 0)
    s = 1
    while s < n:
        x = x + jnp.where(row >= s, pltpu.roll(x, s, 0), 0.0)
        s *= 2
    return x


def _causal_conv(val, cbuf, slot, w, taps, seq_start):
    tm = val.shape[0]

    @pl.when(seq_start)
    def _():
        cbuf[slot, 0:CONV_HALO, :] = jnp.zeros((CONV_HALO, val.shape[1]), F32)

    @pl.when(jnp.logical_not(seq_start))
    def _():
        cbuf[slot, 0:CONV_HALO, :] = cbuf[slot, tm:tm + CONV_HALO, :]

    cbuf[slot, CONV_HALO:CONV_HALO + tm, :] = val
    out = val * w[taps - 1:taps, :]
    for k in range(taps - 1):
        sh = taps - 1 - k
        out = out + cbuf[slot, CONV_HALO - sh:CONV_HALO - sh + tm, :] * w[k:k + 1, :]
    return out


def _sconv_kernel(x_ref, wb_ref, wc_ref, wh_ref, cw_ref, o_ref, cbuf, *, tiles_per_seq):
    seq_start = pl.program_id(1) % tiles_per_seq == 0
    x = x_ref[...]
    b = _mm(x, wb_ref[...])
    ch = _mm(x, wc_ref[...]) * _mm(x, wh_ref[...])
    y = _causal_conv(ch, cbuf, 0, cw_ref[...], 3, seq_start)
    o_ref[...] = (b * y).astype(o_ref.dtype)


def _convsilu_kernel(x_ref, w_ref, cw_ref, cb_ref, o_ref, cbuf, *, tiles_per_seq, taps):
    seq_start = pl.program_id(1) % tiles_per_seq == 0
    v = _mm(x_ref[...], w_ref[...])
    y = _causal_conv(v, cbuf, 0, cw_ref[...], taps, seq_start) + cb_ref[...]
    o_ref[...] = (y * _sigmoid(y)).astype(o_ref.dtype)


def _plain_kernel(x_ref, w_ref, o_ref):
    o_ref[...] = _mm(x_ref[...], w_ref[...]).astype(o_ref.dtype)


def _dt_kernel(x_ref, w_ref, b_ref, o_ref):
    o_ref[...] = _softplus(_mm(x_ref[...], w_ref[...]) + b_ref[...])


def _ffn_up_kernel(x_ref, wu_ref, wg_ref, cwu_ref, cwg_ref, cbu_ref, cbg_ref, o_ref, cbuf, *, tiles_per_seq):
    seq_start = pl.program_id(1) % tiles_per_seq == 0
    x = x_ref[...]
    u = _causal_conv(_mm(x, wu_ref[...]), cbuf, 0, cwu_ref[...], 3, seq_start) + cbu_ref[...]
    g = _causal_conv(_mm(x, wg_ref[...]), cbuf, 1, cwg_ref[...], 3, seq_start) + cbg_ref[...]
    o_ref[...] = (u * (g * _sigmoid(g))).astype(o_ref.dtype)


def _col_spec(rows, tn, off_blocks):
    return pl.BlockSpec((rows, tn), lambda j, i, o=off_blocks: (0, o + j))


def _proj_call(body, xb, col_inputs, out_cols, out_dtype, *, tm, tn, n_conv, seq, name):
    t, d = xb.shape
    assert t % tm == 0 and out_cols % tn == 0 and seq % tm == 0
    in_specs = [pl.BlockSpec((tm, d), lambda j, i: (i, 0))]
    args = [xb]
    for arr, off in col_inputs:
        assert off % tn == 0
        in_specs.append(_col_spec(arr.shape[0], tn, off // tn))
        args.append(arr)
    scratch = [pltpu.VMEM((n_conv, tm + CONV_HALO, tn), F32)] if n_conv else []
    return pl.pallas_call(
        body,
        grid=(out_cols // tn, t // tm),
        in_specs=in_specs,
        out_specs=pl.BlockSpec((tm, tn), lambda j, i: (i, j)),
        out_shape=jax.ShapeDtypeStruct((t, out_cols), out_dtype),
        scratch_shapes=scratch,
        compiler_params=_params(("arbitrary", "arbitrary")),
        name=name,
    )(*args)


def _outproj_ln_kernel(*refs, n_in):
    a_refs = refs[:n_in]
    w_refs = refs[n_in:2 * n_in]
    xres_ref, g_ref, b_ref, o32_ref, o16_ref = refs[2 * n_in:]
    acc = _mm(a_refs[0][...], w_refs[0][...])
    for a_ref, w_ref in zip(a_refs[1:], w_refs[1:]):
        acc = acc + _mm(a_ref[...], w_ref[...])
    y = ALPHA * xres_ref[...] + acc
    mu = jnp.mean(y, axis=-1, keepdims=True)
    d = y - mu
    var = jnp.mean(d * d, axis=-1, keepdims=True)
    out = d * lax.rsqrt(var + LN_EPS) * g_ref[...] + b_ref[...]
    o32_ref[...] = out
    o16_ref[...] = out.astype(BF16)


def _outproj_ln(acts, weights, xres, g, b, *, tm, name):
    t, d = xres.shape
    in_specs = [pl.BlockSpec((tm, a.shape[1]), lambda i: (i, 0)) for a in acts]
    in_specs += [pl.BlockSpec(w.shape, lambda i: (0, 0)) for w in weights]
    in_specs += [pl.BlockSpec((tm, d), lambda i: (i, 0)),
                 pl.BlockSpec((1, d), lambda i: (0, 0)),
                 pl.BlockSpec((1, d), lambda i: (0, 0))]
    return pl.pallas_call(
        functools.partial(_outproj_ln_kernel, n_in=len(acts)),
        grid=(t // tm,),
        in_specs=in_specs,
        out_specs=[pl.BlockSpec((tm, d), lambda i: (i, 0)), pl.BlockSpec((tm, d), lambda i: (i, 0))],
        out_shape=[jax.ShapeDtypeStruct((t, d), F32), jax.ShapeDtypeStruct((t, d), BF16)],
        compiler_params=_params(("arbitrary",)),
        name=name,
    )(*acts, *weights, xres, g.reshape(1, d), b.reshape(1, d))


def _ssd_kernel(xs_ref, bm_ref, cm_ref, dt_ref, z_ref, alog_ref, dskip_ref, ng_ref, o_ref, state):
    q = SSM_CHUNK
    gw = (SSM_HEADS // SSM_GROUPS) * SSM_HEADDIM
    pairs_per_group = gw // LANES

    @pl.when(pl.program_id(1) == 0)
    def _():
        state[...] = jnp.zeros(state.shape, F32)

    lane_row = lax.broadcasted_iota(jnp.int32, (1, LANES), 1)
    first_head = lane_row < SSM_HEADDIM
    dt = dt_ref[...]
    a_row = jnp.where(lane_row < SSM_HEADS, -jnp.exp(alog_ref[...]), 0.0)
    acs = _cumsum_rows(dt * a_row)
    acs_t = acs.T
    tot = acs[q - 1:q, :]
    row = lax.broadcasted_iota(jnp.int32, (q, q), 0)
    col = lax.broadcasted_iota(jnp.int32, (q, q), 1)
    causal = row >= col

    for g in range(SSM_GROUPS):
        bmat = bm_ref[:, g * SSM_STATE:(g + 1) * SSM_STATE]
        cmat = cm_ref[:, g * SSM_STATE:(g + 1) * SSM_STATE]
        cb = lax.dot_general(cmat, bmat, (((1,), (1,)), ((), ())), preferred_element_type=F32)
        prev_t = state[g]
        y_off = _mm(cmat, prev_t.astype(BF16))
        xd_parts, dec_parts, u_parts = [], [], []
        for p in range(pairs_per_group):
            c0 = g * gw + p * LANES
            h0 = c0 // SSM_HEADDIM
            x2 = xs_ref[:, c0:c0 + LANES]
            acol = [acs[:, h0 + e:h0 + e + 1] for e in (0, 1)]
            dt2 = jnp.where(first_head, dt[:, h0:h0 + 1], dt[:, h0 + 1:h0 + 2])
            xdt2 = x2 * dt2
            xdt2_b = xdt2.astype(BF16)
            ydiag = []
            for e in (0, 1):
                seg = jnp.exp(jnp.where(causal, acol[e] - acs_t[h0 + e:h0 + e + 1, :], -jnp.inf))
                ydiag.append(_mm((cb * seg).astype(BF16), xdt2_b))
            grow2 = jnp.where(first_head, jnp.exp(acol[0]), jnp.exp(acol[1]))
            y2 = jnp.where(first_head, ydiag[0], ydiag[1]) + y_off[:, p * LANES:(p + 1) * LANES] * grow2
            dend2 = jnp.where(first_head,
                              jnp.exp(tot[:, h0:h0 + 1] - acol[0]),
                              jnp.exp(tot[:, h0 + 1:h0 + 2] - acol[1]))
            xd_parts.append((xdt2 * dend2).astype(BF16))
            dec_parts.append(jnp.where(first_head, jnp.exp(tot[:, h0:h0 + 1]), jnp.exp(tot[:, h0 + 1:h0 + 2])))
            y2 = y2 + dskip_ref[:, c0:c0 + LANES] * x2
            zz = z_ref[:, c0:c0 + LANES]
            u_parts.append(y2 * (zz * _sigmoid(zz)))
        xd = jnp.concatenate(xd_parts, axis=1)
        new_t = lax.dot_general(bmat, xd, (((0,), (0,)), ((), ())), preferred_element_type=F32)
        state[g] = prev_t * jnp.concatenate(dec_parts, axis=1) + new_t
        u = jnp.concatenate(u_parts, axis=1)
        ms = jnp.mean(u * u, axis=-1, keepdims=True)
        o_ref[:, g * gw:(g + 1) * gw] = (u * lax.rsqrt(ms + LN_EPS) * ng_ref[:, g * gw:(g + 1) * gw]).astype(o_ref.dtype)


def _ssd_call(xs, bm, cm, dt, z, alog_row, dskip_row, ng_row, *, batch, seq):
    t, inner = xs.shape
    nc = seq // SSM_CHUNK
    q = SSM_CHUNK
    gw = inner // SSM_GROUPS
    rows = lambda b, c: (b * nc + c, 0)
    const = lambda b, c: (0, 0)
    return pl.pallas_call(
        _ssd_kernel,
        grid=(batch, nc),
        in_specs=[pl.BlockSpec((q, inner), rows),
                  pl.BlockSpec((q, bm.shape[1]), rows),
                  pl.BlockSpec((q, cm.shape[1]), rows),
                  pl.BlockSpec((q, LANES), rows),
                  pl.BlockSpec((q, inner), rows),
                  pl.BlockSpec((1, LANES), const),
                  pl.BlockSpec((1, inner), const),
                  pl.BlockSpec((1, inner), const)],
        out_specs=pl.BlockSpec((q, inner), rows),
        out_shape=jax.ShapeDtypeStruct((t, inner), BF16),
        scratch_shapes=[pltpu.VMEM((SSM_GROUPS, SSM_STATE, gw), F32)],
        compiler_params=_params(("arbitrary", "arbitrary")),
        name="ssd_scan",
    )(xs, bm, cm, dt, z, alog_row, dskip_row, ng_row)


def _split3(x):
    hi = x.astype(BF16)
    r1 = x - hi.astype(F32)
    mid = r1.astype(BF16)
    lo = (r1 - mid.astype(F32)).astype(BF16)
    return hi, mid, lo


def _fox_gate_kernel(x_ref, w_ref, b_ref, eq_ref, ek_ref):
    f = _mm(x_ref[...], w_ref[...]) + b_ref[...]
    cum = _cumsum_rows(-_softplus(-f))
    parts = _split3(cum)
    r = lax.broadcasted_iota(jnp.int32, (LANES, LANES), 0)
    l = lax.broadcasted_iota(jnp.int32, (LANES, LANES), 1)
    lane = lax.broadcasted_iota(jnp.int32, (1, LANES), 1) & (FOX_EXT - 1)
    spread_q = spread_k = None
    for c, part in enumerate(parts):
        sq = _mm(part, jnp.where(l == FOX_EXT * r + c, 1.0, 0.0).astype(BF16))
        sk = _mm(part, jnp.where(l == FOX_EXT * r + 3 + c, 1.0, 0.0).astype(BF16))
        spread_q = sq if spread_q is None else spread_q + sq
        spread_k = sk if spread_k is None else spread_k + sk
    ones_q = jnp.where((lane >= 3) & (lane < 6), 1.0, 0.0)
    ones_k = jnp.where(lane < 3, 1.0, 0.0)
    eq_ref[...] = (spread_q + ones_q).astype(BF16)
    ek_ref[...] = (ones_k - spread_k).astype(BF16)


def _fox_gate_call(xb, wf, bf, *, batch, seq):
    t, d = xb.shape
    return pl.pallas_call(
        _fox_gate_kernel,
        grid=(batch,),
        in_specs=[pl.BlockSpec((seq, d), lambda b: (b, 0)),
                  pl.BlockSpec((d, LANES), lambda b: (0, 0)),
                  pl.BlockSpec((1, LANES), lambda b: (0, 0))],
        out_specs=[pl.BlockSpec((seq, LANES), lambda b: (b, 0)), pl.BlockSpec((seq, LANES), lambda b: (b, 0))],
        out_shape=[jax.ShapeDtypeStruct((t, LANES), BF16), jax.ShapeDtypeStruct((t, LANES), BF16)],
        compiler_params=_params(("arbitrary",)),
        name="fox_gate",
    )(xb, wf, bf)


def _fox_attn_kernel(q_ref, k_ref, v_ref, eq_ref, ek_ref, o_ref, *, tq, nq):
    hp = pl.program_id(1)
    lane2 = lax.broadcasted_iota(jnp.int32, (1, 2 * LANES), 1)
    head_lanes = []
    for e in (0, 1):
        ext0 = LANES + FOX_EXT * (2 * hp + e)
        head_lanes.append(((lane2 >= FOX_HEADDIM * e) & (lane2 < FOX_HEADDIM * (e + 1)))
                          | ((lane2 >= ext0) & (lane2 < ext0 + FOX_EXT)))
    first_head = lax.broadcasted_iota(jnp.int32, (1, LANES), 1) < FOX_HEADDIM
    row = lax.broadcasted_iota(jnp.int32, (tq, tq), 0)
    col = lax.broadcasted_iota(jnp.int32, (tq, tq), 1)
    on_or_below_diag = row >= col
    scale = FOX_HEADDIM ** -0.5

    def q_block(i, _):
        r0 = pl.multiple_of(i * tq, tq)
        qe = jnp.concatenate([q_ref[pl.ds(r0, tq), :] * scale, eq_ref[pl.ds(r0, tq), :]], axis=1)
        qm = [jnp.where(m, qe, jnp.zeros_like(qe)) for m in head_lanes]

        def kv_step(j, carry, diagonal):
            c0 = pl.multiple_of(j * tq, tq)
            ke = jnp.concatenate([k_ref[pl.ds(c0, tq), :], ek_ref[pl.ds(c0, tq), :]], axis=1)
            vv = v_ref[pl.ds(c0, tq), :]
            out = []
            for e in (0, 1):
                m, l, acc = carry[e]
                s = lax.dot_general(qm[e], ke, (((1,), (1,)), ((), ())), preferred_element_type=F32)
                if diagonal:
                    s = jnp.where(on_or_below_diag, s, NEG_BIG)
                m_new = jnp.maximum(m, jnp.max(s, axis=1, keepdims=True))
                alpha = jnp.exp(m - m_new)
                p = jnp.exp(s - m_new)
                l = alpha * l + jnp.sum(p, axis=1, keepdims=True)
                acc = alpha * acc + _mm(p.astype(BF16), vv)
                out.append((m_new, l, acc))
            return tuple(out)

        init = tuple((jnp.full((tq, 1), NEG_BIG, F32), jnp.zeros((tq, 1), F32), jnp.zeros((tq, LANES), F32))
                     for _ in (0, 1))
        carry = lax.fori_loop(0, i, lambda j, c: kv_step(j, c, False), init)
        carry = kv_step(i, carry, True)
        o = jnp.where(first_head, carry[0][2] / carry[0][1], carry[1][2] / carry[1][1])
        o_ref[pl.ds(r0, tq), :] = o.astype(o_ref.dtype)
        return 0

    lax.fori_loop(0, nq, q_block, 0)


def _fox_attn_call(qkv, eq, ek, *, batch, seq, tq):
    t = qkv.shape[0]
    width = FOX_HEADS * FOX_HEADDIM
    npairs = width // LANES
    blk = lambda off: pl.BlockSpec((seq, LANES), lambda b, hp, o=off: (b, o + hp))
    ext = pl.BlockSpec((seq, LANES), lambda b, hp: (b, 0))
    return pl.pallas_call(
        functools.partial(_fox_attn_kernel, tq=tq, nq=seq // tq),
        grid=(batch, npairs),
        in_specs=[blk(0), blk(npairs), blk(2 * npairs), ext, ext],
        out_specs=pl.BlockSpec((seq, LANES), lambda b, hp: (b, hp)),
        out_shape=jax.ShapeDtypeStruct((t, width), BF16),
        compiler_params=_params(("arbitrary", "arbitrary")),
        name="fox_attn",
    )(qkv, qkv, qkv, eq, ek)


def _conv_ffn(x32, xb, w_up, conv_w, conv_b, w_down, g, b, *, seq, name):
    d_ff = w_down.shape[0]
    tn = d_ff // 2
    cb = conv_b.reshape(1, -1)
    act = _proj_call(
        functools.partial(_ffn_up_kernel, tiles_per_seq=seq // 512),
        xb, [(w_up, 0), (w_up, d_ff), (conv_w, 0), (conv_w, d_ff), (cb, 0), (cb, d_ff)],
        d_ff, BF16, tm=512, tn=tn, n_conv=2, seq=seq, name=name + "_up")
    return _outproj_ln([act], [w_down], x32, g, b, tm=512, name=name + "_down")


def _pad_cols(a, width):
    return jnp.pad(a, ((0, 0), (0, width - a.shape[1])))


def kernel(x, sc_ssm_w_in, sc_conv_w, ssm_conv_w, ssm_conv_b, ssm_dt_bias, ssm_a_log, ssm_d, ssm_norm_g,
           sc_ssm_w_out, fox_w_in, fox_b_f, fox_w_out, ffn_w_up, ffn_conv_w, ffn_conv_b, ffn_w_down,
           ln_mix_g, ln_mix_b, ln_ffn_g, ln_ffn_b):
    batch, seq, d = x.shape
    t = batch * seq
    x32 = x.reshape(t, d)
    xb = x32.astype(BF16)

    w_in = sc_ssm_w_in[0].astype(BF16)
    inner = SSM_HEADS * SSM_HEADDIM
    bc_w = SSM_GROUPS * SSM_STATE
    o_z, o_x, o_b, o_c, o_dt = 3 * d, 3 * d + inner, 3 * d + 2 * inner, 3 * d + 2 * inner + bc_w, 3 * d + 2 * inner + 2 * bc_w
    tps = seq // 512
    ya = _proj_call(functools.partial(_sconv_kernel, tiles_per_seq=tps), xb,
                    [(w_in, 0), (w_in, d), (w_in, 2 * d), (sc_conv_w[0], 0)],
                    d, BF16, tm=512, tn=d, n_conv=1, seq=seq, name="l0_sconv")
    z = _proj_call(_plain_kernel, xb, [(w_in, o_z)], inner, F32, tm=512, tn=inner, n_conv=0, seq=seq, name="l0_z")
    cw, cb = ssm_conv_w[0], ssm_conv_b[0].reshape(1, -1)
    convsilu = functools.partial(_convsilu_kernel, tiles_per_seq=tps, taps=cw.shape[0])
    xs = _proj_call(convsilu, xb, [(w_in, o_x), (cw, 0), (cb, 0)], inner, F32,
                    tm=512, tn=inner, n_conv=1, seq=seq, name="l0_xs")
    bm = _proj_call(convsilu, xb, [(w_in, o_b), (cw, inner), (cb, inner)], bc_w, BF16,
                    tm=512, tn=bc_w, n_conv=1, seq=seq, name="l0_bm")
    cm = _proj_call(convsilu, xb, [(w_in, o_c), (cw, inner + bc_w), (cb, inner + bc_w)], bc_w, BF16,
                    tm=512, tn=bc_w, n_conv=1, seq=seq, name="l0_cm")
    w_dt = _pad_cols(sc_ssm_w_in[0][:, o_dt:], LANES).astype(BF16)
    dt_b = _pad_cols(ssm_dt_bias[0].reshape(1, -1), LANES)
    dt = _proj_call(_dt_kernel, xb, [(w_dt, 0), (dt_b, 0)], LANES, F32, tm=512, tn=LANES, n_conv=0, seq=seq, name="l0_dt")
    yb = _ssd_call(xs, bm, cm, dt, z,
                   _pad_cols(ssm_a_log[0].reshape(1, -1), LANES),
                   jnp.repeat(ssm_d[0], SSM_HEADDIM).reshape(1, -1),
                   ssm_norm_g[0].reshape(1, -1), batch=batch, seq=seq)
    w_out = sc_ssm_w_out[0].astype(BF16)
    x32, xb = _outproj_ln([ya, yb], [w_out[:d], w_out[d:]], x32, ln_mix_g[0], ln_mix_b[0], tm=512, name="l0_out")
    x32, xb = _conv_ffn(x32, xb, ffn_w_up[0].astype(BF16), ffn_conv_w[0], ffn_conv_b[0],
                        ffn_w_down[0].astype(BF16), ln_ffn_g[0], ln_ffn_b[0], seq=seq, name="l0_ffn")

    width = FOX_HEADS * FOX_HEADDIM
    fw = fox_w_in[0]
    qkv = _proj_call(_plain_kernel, xb, [(fw[:, :3 * width].astype(BF16), 0)], 3 * width, BF16,
                     tm=512, tn=width, n_conv=0, seq=seq, name="l1_qkv")
    eq, ek = _fox_gate_call(xb, _pad_cols(fw[:, 3 * width:], LANES).astype(BF16),
                            _pad_cols(fox_b_f[0].reshape(1, -1), LANES), batch=batch, seq=seq)
    o = _fox_attn_call(qkv, eq, ek, batch=batch, seq=seq, tq=256)
    x32, xb = _outproj_ln([o], [fox_w_out[0].astype(BF16)], x32, ln_mix_g[1], ln_mix_b[1], tm=512, name="l1_out")
    x32, xb = _conv_ffn(x32, xb, ffn_w_up[1].astype(BF16), ffn_conv_w[1], ffn_conv_b[1],
                        ffn_w_down[1].astype(BF16), ln_ffn_g[1], ln_ffn_b[1], seq=seq, name="l1_ffn")
    return x32.reshape(batch, seq, d)
```

```python
import functools

import jax
import jax.numpy as jnp
from jax import lax
from jax.experimental import pallas as pl
from jax.experimental.pallas import tpu as pltpu

F32 = jnp.float32
BF16 = jnp.bfloat16

DEPTH = 2
ALPHA = (2 * DEPTH) ** 0.25
LN_EPS = 1e-5

SSM_HEADDIM = 64
SSM_HEADS = 16
SSM_GROUPS = 2
SSM_STATE = 128
SSM_CHUNK = 128

FOX_HEADS = 16
FOX_HEADDIM = 64
FOX_EXT = 8

LANES = 128
CONV_HALO = 8
NEG_BIG = -1e30
VMEM_LIMIT = 56 * 1024 * 1024


def _params(sem):
    return pltpu.CompilerParams(dimension_semantics=sem, vmem_limit_bytes=VMEM_LIMIT)


def _mm(a, b):
    return jnp.dot(a, b, preferred_element_type=F32)


def _sigmoid(x):
    return 1.0 / (1.0 + jnp.exp(-x))


def _softplus(x):
    return jnp.maximum(x, 0.0) + jnp.log1p(jnp.exp(-jnp.abs(x)))


def _cumsum_rows(x):
    n = x.shape[0]
    row = lax.broadcasted_iota(jnp.int32, x.shape, 0)
    s = 1
    while s < n:
        x = x + jnp.where(row >= s, pltpu.roll(x, s, 0), 0.0)
        s *= 2
    return x


def _causal_conv(val, cbuf, slot, w, taps, seq_start):
    tm = val.shape[0]

    @pl.when(seq_start)
    def _():
        cbuf[slot, 0:CONV_HALO, :] = jnp.zeros((CONV_HALO, val.shape[1]), F32)

    @pl.when(jnp.logical_not(seq_start))
    def _():
        cbuf[slot, 0:CONV_HALO, :] = cbuf[slot, tm:tm + CONV_HALO, :]

    cbuf[slot, CONV_HALO:CONV_HALO + tm, :] = val
    out = val * w[taps - 1:taps, :]
    for k in range(taps - 1):
        sh = taps - 1 - k
        out = out + cbuf[slot, CONV_HALO - sh:CONV_HALO - sh + tm, :] * w[k:k + 1, :]
    return out


def _sconv_kernel(x_ref, wb_ref, wc_ref, wh_ref, cw_ref, o_ref, cbuf, *, tiles_per_seq):
    seq_start = pl.program_id(1) % tiles_per_seq == 0
    x = x_ref[...]
    b = _mm(x, wb_ref[...])
    ch = _mm(x, wc_ref[...]) * _mm(x, wh_ref[...])
    y = _causal_conv(ch, cbuf, 0, cw_ref[...], 3, seq_start)
    o_ref[...] = (b * y).astype(o_ref.dtype)


def _convsilu_kernel(x_ref, w_ref, cw_ref, cb_ref, o_ref, cbuf, *, tiles_per_seq, taps):
    seq_start = pl.program_id(1) % tiles_per_seq == 0
    v = _mm(x_ref[...], w_ref[...])
    y = _causal_conv(v, cbuf, 0, cw_ref[...], taps, seq_start) + cb_ref[...]
    o_ref[...] = (y * _sigmoid(y)).astype(o_ref.dtype)


def _plain_kernel(x_ref, w_ref, o_ref):
    o_ref[...] = _mm(x_ref[...], w_ref[...]).astype(o_ref.dtype)


def _dt_kernel(x_ref, w_ref, b_ref, o_ref):
    o_ref[...] = _softplus(_mm(x_ref[...], w_ref[...]) + b_ref[...])


def _ffn_up_kernel(x_ref, wu_ref, wg_ref, cwu_ref, cwg_ref, cbu_ref, cbg_ref, o_ref, cbuf, *, tiles_per_seq):
    seq_start = pl.program_id(1) % tiles_per_seq == 0
    x = x_ref[...]
    u = _causal_conv(_mm(x, wu_ref[...]), cbuf, 0, cwu_ref[...], 3, seq_start) + cbu_ref[...]
    g = _causal_conv(_mm(x, wg_ref[...]), cbuf, 1, cwg_ref[...], 3, seq_start) + cbg_ref[...]
    o_ref[...] = (u * (g * _sigmoid(g))).astype(o_ref.dtype)


def _col_spec(rows, tn, off_blocks):
    return pl.BlockSpec((rows, tn), lambda j, i, o=off_blocks: (0, o + j))


def _proj_call(body, xb, col_inputs, out_cols, out_dtype, *, tm, tn, n_conv, seq, name):
    t, d = xb.shape
    assert t % tm == 0 and out_cols % tn == 0 and seq % tm == 0
    in_specs = [pl.BlockSpec((tm, d), lambda j, i: (i, 0))]
    args = [xb]
    for arr, off in col_inputs:
        assert off % tn == 0
        in_specs.append(_col_spec(arr.shape[0], tn, off // tn))
        args.append(arr)
    scratch = [pltpu.VMEM((n_conv, tm + CONV_HALO, tn), F32)] if n_conv else []
    return pl.pallas_call(
        body,
        grid=(out_cols // tn, t // tm),
        in_specs=in_specs,
        out_specs=pl.BlockSpec((tm, tn), lambda j, i: (i, j)),
        out_shape=jax.ShapeDtypeStruct((t, out_cols), out_dtype),
        scratch_shapes=scratch,
        compiler_params=_params(("arbitrary", "arbitrary")),
        name=name,
    )(*args)


def _outproj_ln_kernel(*refs, n_in):
    a_refs = refs[:n_in]
    w_refs = refs[n_in:2 * n_in]
    xres_ref, g_ref, b_ref, o32_ref, o16_ref = refs[2 * n_in:]
    acc = _mm(a_refs[0][...], w_refs[0][...])
    for a_ref, w_ref in zip(a_refs[1:], w_refs[1:]):
        acc = acc + _mm(a_ref[...], w_ref[...])
    y = ALPHA * xres_ref[...] + acc
    mu = jnp.mean(y, axis=-1, keepdims=True)
    d = y - mu
    var = jnp.mean(d * d, axis=-1, keepdims=True)
    out = d * lax.rsqrt(var + LN_EPS) * g_ref[...] + b_ref[...]
    o32_ref[...] = out
    o16_ref[...] = out.astype(BF16)


def _outproj_ln(acts, weights, xres, g, b, *, tm, name):
    t, d = xres.shape
    in_specs = [pl.BlockSpec((tm, a.shape[1]), lambda i: (i, 0)) for a in acts]
    in_specs += [pl.BlockSpec(w.shape, lambda i: (0, 0)) for w in weights]
    in_specs += [pl.BlockSpec((tm, d), lambda i: (i, 0)),
                 pl.BlockSpec((1, d), lambda i: (0, 0)),
                 pl.BlockSpec((1, d), lambda i: (0, 0))]
    return pl.pallas_call(
        functools.partial(_outproj_ln_kernel, n_in=len(acts)),
        grid=(t // tm,),
        in_specs=in_specs,
        out_specs=[pl.BlockSpec((tm, d), lambda i: (i, 0)), pl.BlockSpec((tm, d), lambda i: (i, 0))],
        out_shape=[jax.ShapeDtypeStruct((t, d), F32), jax.ShapeDtypeStruct((t, d), BF16)],
        compiler_params=_params(("arbitrary",)),
        name=name,
    )(*acts, *weights, xres, g.reshape(1, d), b.reshape(1, d))


def _ssd_kernel(xs_ref, bm_ref, cm_ref, dt_ref, z_ref, alog_ref, dskip_ref, ng_ref, o_ref, state):
    q = SSM_CHUNK
    gw = (SSM_HEADS // SSM_GROUPS) * SSM_HEADDIM
    pairs_per_group = gw // LANES

    @pl.when(pl.program_id(1) == 0)
    def _():
        state[...] = jnp.zeros(state.shape, F32)

    lane_row = lax.broadcasted_iota(jnp.int32, (1, LANES), 1)
    first_head = lane_row < SSM_HEADDIM
    dt = dt_ref[...]
    a_row = jnp.where(lane_row < SSM_HEADS, -jnp.exp(alog_ref[...]), 0.0)
    acs = _cumsum_rows(dt * a_row)
    acs_t = acs.T
    tot = acs[q - 1:q, :]
    row = lax.broadcasted_iota(jnp.int32, (q, q), 0)
    col = lax.broadcasted_iota(jnp.int32, (q, q), 1)
    causal = row >= col

    for g in range(SSM_GROUPS):
        bmat = bm_ref[:, g * SSM_STATE:(g + 1) * SSM_STATE]
        cmat = cm_ref[:, g * SSM_STATE:(g + 1) * SSM_STATE]
        cb = lax.dot_general(cmat, bmat, (((1,), (1,)), ((), ())), preferred_element_type=F32)
        prev_t = state[g]
        y_off = _mm(cmat, prev_t.astype(BF16))
        xd_parts, dec_parts, u_parts = [], [], []
        for p in range(pairs_per_group):
            c0 = g * gw + p * LANES
            h0 = c0 // SSM_HEADDIM
            x2 = xs_ref[:, c0:c0 + LANES]
            acol = [acs[:, h0 + e:h0 + e + 1] for e in (0, 1)]
            dt2 = jnp.where(first_head, dt[:, h0:h0 + 1], dt[:, h0 + 1:h0 + 2])
            xdt2 = x2 * dt2
            xdt2_b = xdt2.astype(BF16)
            ydiag = []
            for e in (0, 1):
                seg = jnp.exp(jnp.where(causal, acol[e] - acs_t[h0 + e:h0 + e + 1, :], -jnp.inf))
                ydiag.append(_mm((cb * seg).astype(BF16), xdt2_b))
            grow2 = jnp.where(first_head, jnp.exp(acol[0]), jnp.exp(acol[1]))
            y2 = jnp.where(first_head, ydiag[0], ydiag[1]) + y_off[:, p * LANES:(p + 1) * LANES] * grow2
            dend2 = jnp.where(first_head,
                              jnp.exp(tot[:, h0:h0 + 1] - acol[0]),
                              jnp.exp(tot[:, h0 + 1:h0 + 2] - acol[1]))
            xd_parts.append((xdt2 * dend2).astype(BF16))
            dec_parts.append(jnp.where(first_head, jnp.exp(tot[:, h0:h0 + 1]), jnp.exp(tot[:, h0 + 1:h0 + 2])))
            y2 = y2 + dskip_ref[:, c0:c0 + LANES] * x2
            zz = z_ref[:, c0:c0 + LANES]
            u_parts.append(y2 * (zz * _sigmoid(zz)))
        xd = jnp.concatenate(xd_parts, axis=1)
        new_t = lax.dot_general(bmat, xd, (((0,), (0,)), ((), ())), preferred_element_type=F32)
        state[g] = prev_t * jnp.concatenate(dec_parts, axis=1) + new_t
        u = jnp.concatenate(u_parts, axis=1)
        ms = jnp.mean(u * u, axis=-1, keepdims=True)
        o_ref[:, g * gw:(g + 1) * gw] = (u * lax.rsqrt(ms + LN_EPS) * ng_ref[:, g * gw:(g + 1) * gw]).astype(o_ref.dtype)


def _ssd_call(xs, bm, cm, dt, z, alog_row, dskip_row, ng_row, *, batch, seq):
    t, inner = xs.shape
    nc = seq // SSM_CHUNK
    q = SSM_CHUNK
    gw = inner // SSM_GROUPS
    rows = lambda b, c: (b * nc + c, 0)
    const = lambda b, c: (0, 0)
    return pl.pallas_call(
        _ssd_kernel,
        grid=(batch, nc),
        in_specs=[pl.BlockSpec((q, inner), rows),
                  pl.BlockSpec((q, bm.shape[1]), rows),
                  pl.BlockSpec((q, cm.shape[1]), rows),
                  pl.BlockSpec((q, LANES), rows),
                  pl.BlockSpec((q, inner), rows),
                  pl.BlockSpec((1, LANES), const),
                  pl.BlockSpec((1, inner), const),
                  pl.BlockSpec((1, inner), const)],
        out_specs=pl.BlockSpec((q, inner), rows),
        out_shape=jax.ShapeDtypeStruct((t, inner), BF16),
        scratch_shapes=[pltpu.VMEM((SSM_GROUPS, SSM_STATE, gw), F32)],
        compiler_params=_params(("arbitrary", "arbitrary")),
        name="ssd_scan",
    )(xs, bm, cm, dt, z, alog_row, dskip_row, ng_row)


def _split3(x):
    hi = x.astype(BF16)
    r1 = x - hi.astype(F32)
    mid = r1.astype(BF16)
    lo = (r1 - mid.astype(F32)).astype(BF16)
    return hi, mid, lo


def _fox_gate_kernel(x_ref, w_ref, b_ref, eq_ref, ek_ref):
    f = _mm(x_ref[...], w_ref[...]) + b_ref[...]
    cum = _cumsum_rows(-_softplus(-f))
    parts = _split3(cum)
    r = lax.broadcasted_iota(jnp.int32, (LANES, LANES), 0)
    l = lax.broadcasted_iota(jnp.int32, (LANES, LANES), 1)
    lane = lax.broadcasted_iota(jnp.int32, (1, LANES), 1) & (FOX_EXT - 1)
    spread_q = spread_k = None
    for c, part in enumerate(parts):
        sq = _mm(part, jnp.where(l == FOX_EXT * r + c, 1.0, 0.0).astype(BF16))
        sk = _mm(part, jnp.where(l == FOX_EXT * r + 3 + c, 1.0, 0.0).astype(BF16))
        spread_q = sq if spread_q is None else spread_q + sq
        spread_k = sk if spread_k is None else spread_k + sk
    ones_q = jnp.where((lane >= 3) & (lane < 6), 1.0, 0.0)
    ones_k = jnp.where(lane < 3, 1.0, 0.0)
    eq_ref[...] = (spread_q + ones_q).astype(BF16)
    ek_ref[...] = (ones_k - spread_k).astype(BF16)


def _fox_gate_call(xb, wf, bf, *, batch, seq):
    t, d = xb.shape
    return pl.pallas_call(
        _fox_gate_kernel,
        grid=(batch,),
        in_specs=[pl.BlockSpec((seq, d), lambda b: (b, 0)),
                  pl.BlockSpec((d, LANES), lambda b: (0, 0)),
                  pl.BlockSpec((1, LANES), lambda b: (0, 0))],
        out_specs=[pl.BlockSpec((seq, LANES), lambda b: (b, 0)), pl.BlockSpec((seq, LANES), lambda b: (b, 0))],
        out_shape=[jax.ShapeDtypeStruct((t, LANES), BF16), jax.ShapeDtypeStruct((t, LANES), BF16)],
        compiler_params=_params(("arbitrary",)),
        name="fox_gate",
    )(xb, wf, bf)


def _fox_attn_kernel(q_ref, k_ref, v_ref, eq_ref, ek_ref, o_ref, vt_ref, *, tq, nq):
    hp = pl.program_id(1)
    lane2 = lax.broadcasted_iota(jnp.int32, (1, 2 * LANES), 1)
    head_lanes = []
    for e in (0, 1):
        ext0 = LANES + FOX_EXT * (2 * hp + e)
        head_lanes.append(((lane2 >= FOX_HEADDIM * e) & (lane2 < FOX_HEADDIM * (e + 1)))
                          | ((lane2 >= ext0) & (lane2 < ext0 + FOX_EXT)))
    key = lax.broadcasted_iota(jnp.int32, (tq, tq), 0)
    qry = lax.broadcasted_iota(jnp.int32, (tq, tq), 1)
    visible = key <= qry
    scale = FOX_HEADDIM ** -0.5

    for j in range(nq):
        vt_ref[j] = v_ref[j * tq:(j + 1) * tq, :].astype(F32).T.astype(BF16)

    def q_block(i, _):
        r0 = pl.multiple_of(i * tq, tq)
        qe = jnp.concatenate([q_ref[pl.ds(r0, tq), :] * scale, eq_ref[pl.ds(r0, tq), :]], axis=1)
        qm = [jnp.where(m, qe, jnp.zeros_like(qe)) for m in head_lanes]

        def logits(j):
            c0 = pl.multiple_of(j * tq, tq)
            ke = jnp.concatenate([k_ref[pl.ds(c0, tq), :], ek_ref[pl.ds(c0, tq), :]], axis=1)
            return [lax.dot_general(ke, qm[e], (((1,), (1,)), ((), ())), preferred_element_type=F32)
                    for e in (0, 1)]

        def softmax_pv(j, s, stats, diagonal):
            if diagonal:
                s = [jnp.where(visible, se, NEG_BIG) for se in s]
            m_new = [jnp.maximum(st[0], jnp.max(se, axis=0, keepdims=True)) for se, st in zip(s, stats)]
            alpha = [jnp.exp(st[0] - mn) for st, mn in zip(stats, m_new)]
            p = [jnp.exp(se - mn) for se, mn in zip(s, m_new)]
            l = [a * st[1] + jnp.sum(pe, axis=0, keepdims=True) for a, st, pe in zip(alpha, stats, p)]
            pv = [_mm(vt_ref[j, FOX_HEADDIM * e:FOX_HEADDIM * (e + 1), :], p[e].astype(BF16))
                  for e in (0, 1)]
            acc = [a * st[2] + pve for a, st, pve in zip(alpha, stats, pv)]
            return tuple(zip(m_new, l, acc))

        def kv_step(j, carry):
            s, stats = carry
            s_next = logits(j + 1)
            return tuple(s_next), softmax_pv(j, s, stats, False)

        init = tuple((jnp.full((1, tq), NEG_BIG, F32), jnp.zeros((1, tq), F32), jnp.zeros((FOX_HEADDIM, tq), F32))
                     for _ in (0, 1))
        s, stats = lax.fori_loop(0, i, kv_step, (tuple(logits(0)), init))
        stats = softmax_pv(i, list(s), stats, True)
        o_t = jnp.concatenate([stats[0][2] / stats[0][1], stats[1][2] / stats[1][1]], axis=0)
        o_ref[pl.ds(r0, tq), :] = o_t.T.astype(o_ref.dtype)
        return 0

    lax.fori_loop(0, nq, q_block, 0)


def _fox_attn_call(qkv, eq, ek, *, batch, seq, tq):
    t = qkv.shape[0]
    width = FOX_HEADS * FOX_HEADDIM
    npairs = width // LANES
    blk = lambda off: pl.BlockSpec((seq, LANES), lambda b, hp, o=off: (b, o + hp))
    ext = pl.BlockSpec((seq, LANES), lambda b, hp: (b, 0))
    return pl.pallas_call(
        functools.partial(_fox_attn_kernel, tq=tq, nq=seq // tq),
        grid=(batch, npairs),
        in_specs=[blk(0), blk(npairs), blk(2 * npairs), ext, ext],
        out_specs=pl.BlockSpec((seq, LANES), lambda b, hp: (b, hp)),
        out_shape=jax.ShapeDtypeStruct((t, width), BF16),
        scratch_shapes=[pltpu.VMEM((seq // tq, LANES, tq), BF16)],
        compiler_params=_params(("arbitrary", "arbitrary")),
        name="fox_attn",
    )(qkv, qkv, qkv, eq, ek)


def _conv_ffn(x32, xb, w_up, conv_w, conv_b, w_down, g, b, *, seq, name):
    d_ff = w_down.shape[0]
    tn = d_ff // 2
    cb = conv_b.reshape(1, -1)
    act = _proj_call(
        functools.partial(_ffn_up_kernel, tiles_per_seq=seq // 512),
        xb, [(w_up, 0), (w_up, d_ff), (conv_w, 0), (conv_w, d_ff), (cb, 0), (cb, d_ff)],
        d_ff, BF16, tm=512, tn=tn, n_conv=2, seq=seq, name=name + "_up")
    return _outproj_ln([act], [w_down], x32, g, b, tm=512, name=name + "_down")


def _pad_cols(a, width):
    return jnp.pad(a, ((0, 0), (0, width - a.shape[1])))


def kernel(x, sc_ssm_w_in, sc_conv_w, ssm_conv_w, ssm_conv_b, ssm_dt_bias, ssm_a_log, ssm_d, ssm_norm_g,
           sc_ssm_w_out, fox_w_in, fox_b_f, fox_w_out, ffn_w_up, ffn_conv_w, ffn_conv_b, ffn_w_down,
           ln_mix_g, ln_mix_b, ln_ffn_g, ln_ffn_b):
    batch, seq, d = x.shape
    t = batch * seq
    x32 = x.reshape(t, d)
    xb = x32.astype(BF16)

    w_in = sc_ssm_w_in[0].astype(BF16)
    inner = SSM_HEADS * SSM_HEADDIM
    bc_w = SSM_GROUPS * SSM_STATE
    o_z, o_x, o_b, o_c, o_dt = 3 * d, 3 * d + inner, 3 * d + 2 * inner, 3 * d + 2 * inner + bc_w, 3 * d + 2 * inner + 2 * bc_w
    tps = seq // 512
    ya = _proj_call(functools.partial(_sconv_kernel, tiles_per_seq=tps), xb,
                    [(w_in, 0), (w_in, d), (w_in, 2 * d), (sc_conv_w[0], 0)],
                    d, BF16, tm=512, tn=d, n_conv=1, seq=seq, name="l0_sconv")
    z = _proj_call(_plain_kernel, xb, [(w_in, o_z)], inner, F32, tm=512, tn=inner, n_conv=0, seq=seq, name="l0_z")
    cw, cb = ssm_conv_w[0], ssm_conv_b[0].reshape(1, -1)
    convsilu = functools.partial(_convsilu_kernel, tiles_per_seq=tps, taps=cw.shape[0])
    xs = _proj_call(convsilu, xb, [(w_in, o_x), (cw, 0), (cb, 0)], inner, F32,
                    tm=512, tn=inner, n_conv=1, seq=seq, name="l0_xs")
    bm = _proj_call(convsilu, xb, [(w_in, o_b), (cw, inner), (cb, inner)], bc_w, BF16,
                    tm=512, tn=bc_w, n_conv=1, seq=seq, name="l0_bm")
    cm = _proj_call(convsilu, xb, [(w_in, o_c), (cw, inner + bc_w), (cb, inner + bc_w)], bc_w, BF16,
                    tm=512, tn=bc_w, n_conv=1, seq=seq, name="l0_cm")
    w_dt = _pad_cols(sc_ssm_w_in[0][:, o_dt:], LANES).astype(BF16)
    dt_b = _pad_cols(ssm_dt_bias[0].reshape(1, -1), LANES)
    dt = _proj_call(_dt_kernel, xb, [(w_dt, 0), (dt_b, 0)], LANES, F32, tm=512, tn=LANES, n_conv=0, seq=seq, name="l0_dt")
    yb = _ssd_call(xs, bm, cm, dt, z,
                   _pad_cols(ssm_a_log[0].reshape(1, -1), LANES),
                   jnp.repeat(ssm_d[0], SSM_HEADDIM).reshape(1, -1),
                   ssm_norm_g[0].reshape(1, -1), batch=batch, seq=seq)
    w_out = sc_ssm_w_out[0].astype(BF16)
    x32, xb = _outproj_ln([ya, yb], [w_out[:d], w_out[d:]], x32, ln_mix_g[0], ln_mix_b[0], tm=512, name="l0_out")
    x32, xb = _conv_ffn(x32, xb, ffn_w_up[0].astype(BF16), ffn_conv_w[0], ffn_conv_b[0],
                        ffn_w_down[0].astype(BF16), ln_ffn_g[0], ln_ffn_b[0], seq=seq, name="l0_ffn")

    width = FOX_HEADS * FOX_HEADDIM
    fw = fox_w_in[0]
    qkv = _proj_call(_plain_kernel, xb, [(fw[:, :3 * width].astype(BF16), 0)], 3 * width, BF16,
                     tm=512, tn=width, n_conv=0, seq=seq, name="l1_qkv")
    eq, ek = _fox_gate_call(xb, _pad_cols(fw[:, 3 * width:], LANES).astype(BF16),
                            _pad_cols(fox_b_f[0].reshape(1, -1), LANES), batch=batch, seq=seq)
    o = _fox_attn_call(qkv, eq, ek, batch=batch, seq=seq, tq=256)
    x32, xb = _outproj_ln([o], [fox_w_out[0].astype(BF16)], x32, ln_mix_g[1], ln_mix_b[1], tm=512, name="l1_out")
    x32, xb = _conv_ffn(x32, xb, ffn_w_up[1].astype(BF16), ffn_conv_w[1], ffn_conv_b[1],
                        ffn_w_down[1].astype(BF16), ln_ffn_g[1], ln_ffn_b[1], seq=seq, name="l1_ffn")
    return x32.reshape(batch, seq, d)
```

```python
import functools

import jax
import jax.numpy as jnp
from jax import lax
from jax.experimental import pallas as pl
from jax.experimental.pallas import tpu as pltpu

F32 = jnp.float32
BF16 = jnp.bfloat16

DEPTH = 2
ALPHA = (2 * DEPTH) ** 0.25
LN_EPS = 1e-5

SSM_HEADDIM = 64
SSM_HEADS = 16
SSM_GROUPS = 2
SSM_STATE = 128
SSM_CHUNK = 128

FOX_HEADS = 16
FOX_HEADDIM = 64
FOX_EXT = 8

LANES = 128
ROW_TILE = 512
COL_CHUNK = 256
CONV_HALO = 8
NEG_BIG = -1e30
VMEM_LIMIT = 56 * 1024 * 1024


def _params(sem):
    return pltpu.CompilerParams(dimension_semantics=sem, vmem_limit_bytes=VMEM_LIMIT)


def _mm(a, b):
    return jnp.dot(a, b, preferred_element_type=F32)


def _sigmoid(x):
    return 1.0 / (1.0 + jnp.exp(-x))


def _softplus(x):
    return jnp.maximum(x, 0.0) + jnp.log1p(jnp.exp(-jnp.abs(x)))


def _cumsum_rows(x):
    n = x.shape[0]
    row = lax.broadcasted_iota(jnp.int32, x.shape, 0)
    s = 1
    while s < n:
        x = x + jnp.where(row >= s, pltpu.roll(x, s, 0), 0.0)
        s *= 2
    return x


def _chunks(width):
    assert width % COL_CHUNK == 0
    return [slice(c, c + COL_CHUNK) for c in range(0, width, COL_CHUNK)]


def _shift(cols, off):
    return slice(cols.start + off, cols.stop + off)


def _causal_conv(val, halo, cbuf, slot, cols, w, taps, seq_start):
    tm = val.shape[0]
    cbuf[slot, 0:CONV_HALO, :] = jnp.where(seq_start, 0.0, halo[:, cols])
    cbuf[slot, CONV_HALO:CONV_HALO + tm, :] = val
    halo[:, cols] = val[tm - CONV_HALO:tm, :]
    out = val * w[taps - 1:taps, :]
    for k in range(taps - 1):
        sh = taps - 1 - k
        out = out + cbuf[slot, CONV_HALO - sh:CONV_HALO - sh + tm, :] * w[k:k + 1, :]
    return out


def _layer_norm(y, g, b):
    mu = jnp.mean(y, axis=-1, keepdims=True)
    d = y - mu
    var = jnp.mean(d * d, axis=-1, keepdims=True)
    return d * lax.rsqrt(var + LN_EPS) * g + b


def _resident(shape):
    return pl.BlockSpec(shape, lambda i: (0,) * len(shape), pipeline_mode=pl.Buffered(1))


def _row_spec(cols):
    return pl.BlockSpec((ROW_TILE, cols), lambda i: (i, 0))


def _l0_in_kernel(x_ref, w_ref, wdt_ref, scw_ref, xcw_ref, xcb_ref, dtb_ref,
                  ya_ref, z_ref, xs_ref, bm_ref, cm_ref, dt_ref,
                  halo_sc, halo_x, cbuf, *, tiles_per_seq, d, inner, bc_w):
    seq_start = pl.program_id(0) % tiles_per_seq == 0
    x = x_ref[...]
    for n, cols in enumerate(_chunks(d)):
        gate = _mm(x, w_ref[:, cols])
        ch = _mm(x, w_ref[:, _shift(cols, d)]) * _mm(x, w_ref[:, _shift(cols, 2 * d)])
        y = _causal_conv(ch, halo_sc, cbuf, n % 2, cols, scw_ref[:, cols], scw_ref.shape[0], seq_start)
        ya_ref[:, cols] = (gate * y).astype(ya_ref.dtype)
    for cols in _chunks(inner):
        z_ref[:, cols] = _mm(x, w_ref[:, _shift(cols, 3 * d)]).astype(z_ref.dtype)
    xbc0 = 3 * d + inner
    for n, cols in enumerate(_chunks(inner + 2 * bc_w)):
        v = _mm(x, w_ref[:, _shift(cols, xbc0)])
        y = _causal_conv(v, halo_x, cbuf, n % 2, cols, xcw_ref[:, cols], xcw_ref.shape[0], seq_start) + xcb_ref[:, cols]
        y = y * _sigmoid(y)
        if cols.stop <= inner:
            xs_ref[:, cols] = y.astype(xs_ref.dtype)
        elif cols.stop <= inner + bc_w:
            bm_ref[:, _shift(cols, -inner)] = y.astype(bm_ref.dtype)
        else:
            cm_ref[:, _shift(cols, -inner - bc_w)] = y.astype(cm_ref.dtype)
    dt_ref[...] = _softplus(_mm(x, wdt_ref[...]) + dtb_ref[...])


def _l0_in_call(xb, w_in, w_dt, sc_cw, x_cw, x_cb, dt_b, *, seq, inner, bc_w):
    t, d = xb.shape
    assert bc_w == COL_CHUNK
    kern = functools.partial(_l0_in_kernel, tiles_per_seq=seq // ROW_TILE, d=d, inner=inner, bc_w=bc_w)
    return pl.pallas_call(
        kern,
        grid=(t // ROW_TILE,),
        in_specs=[_row_spec(d), _resident(w_in.shape), _resident(w_dt.shape), _resident(sc_cw.shape),
                  _resident(x_cw.shape), _resident(x_cb.shape), _resident(dt_b.shape)],
        out_specs=[_row_spec(d), _row_spec(inner), _row_spec(inner), _row_spec(bc_w), _row_spec(bc_w), _row_spec(LANES)],
        out_shape=[jax.ShapeDtypeStruct((t, d), BF16), jax.ShapeDtypeStruct((t, inner), F32),
                   jax.ShapeDtypeStruct((t, inner), F32), jax.ShapeDtypeStruct((t, bc_w), BF16),
                   jax.ShapeDtypeStruct((t, bc_w), BF16), jax.ShapeDtypeStruct((t, LANES), F32)],
        scratch_shapes=[pltpu.VMEM((CONV_HALO, d), F32), pltpu.VMEM((CONV_HALO, inner + 2 * bc_w), F32),
                        pltpu.VMEM((2, ROW_TILE + CONV_HALO, COL_CHUNK), F32)],
        compiler_params=_params(("arbitrary",)),
        name="l0_in",
    )(xb, w_in, w_dt, sc_cw, x_cw, x_cb, dt_b)


def _outproj_ln_kernel(*refs, n_in):
    a_refs = refs[:n_in]
    w_refs = refs[n_in:2 * n_in]
    xres_ref, g_ref, b_ref, o32_ref, o16_ref = refs[2 * n_in:]
    acc = _mm(a_refs[0][...], w_refs[0][...])
    for a_ref, w_ref in zip(a_refs[1:], w_refs[1:]):
        acc = acc + _mm(a_ref[...], w_ref[...])
    out = _layer_norm(ALPHA * xres_ref[...] + acc, g_ref[...], b_ref[...])
    o32_ref[...] = out
    o16_ref[...] = out.astype(BF16)


def _outproj_ln(acts, weights, xres, g, b, *, name):
    t, d = xres.shape
    in_specs = [_row_spec(a.shape[1]) for a in acts]
    in_specs += [_resident(w.shape) for w in weights]
    in_specs += [_row_spec(d), _resident((1, d)), _resident((1, d))]
    return pl.pallas_call(
        functools.partial(_outproj_ln_kernel, n_in=len(acts)),
        grid=(t // ROW_TILE,),
        in_specs=in_specs,
        out_specs=[_row_spec(d), _row_spec(d)],
        out_shape=[jax.ShapeDtypeStruct((t, d), F32), jax.ShapeDtypeStruct((t, d), BF16)],
        compiler_params=_params(("arbitrary",)),
        name=name,
    )(*acts, *weights, xres, g.reshape(1, d), b.reshape(1, d))


def _ffn_kernel(x_ref, xres_ref, wup_ref, cw_ref, cb_ref, wd_ref, g_ref, b_ref, o32_ref, o16_ref,
                halo, cbuf, acc, *, tiles_per_seq, d_ff):
    seq_start = pl.program_id(0) % tiles_per_seq == 0
    taps = cw_ref.shape[0]
    chunks = _chunks(d_ff)

    def up(cols):
        return _mm(x_ref[...], wup_ref[:, cols]), _mm(x_ref[...], wup_ref[:, _shift(cols, d_ff)])

    def add_down(act, n):
        part = _mm(act, wd_ref[chunks[n], :])
        if n == 0:
            acc[...] = part
        else:
            acc[...] += part

    ahead = up(chunks[0])
    act_prev = None
    for n, cols in enumerate(chunks):
        gcols = _shift(cols, d_ff)
        u_raw, g_raw = ahead
        if act_prev is not None:
            add_down(act_prev, n - 1)
        if n + 1 < len(chunks):
            ahead = up(chunks[n + 1])
        u = _causal_conv(u_raw, halo, cbuf, 2 * (n % 2), cols, cw_ref[:, cols], taps, seq_start) + cb_ref[:, cols]
        gt = _causal_conv(g_raw, halo, cbuf, 2 * (n % 2) + 1, gcols, cw_ref[:, gcols], taps, seq_start) + cb_ref[:, gcols]
        act_prev = (u * (gt * _sigmoid(gt))).astype(BF16)
    add_down(act_prev, len(chunks) - 1)
    out = _layer_norm(ALPHA * xres_ref[...] + acc[...], g_ref[...], b_ref[...])
    o32_ref[...] = out
    o16_ref[...] = out.astype(BF16)


def _conv_ffn(x32, xb, w_up, conv_w, conv_b, w_down, g, b, *, seq, name):
    t, d = x32.shape
    d_ff = w_down.shape[0]
    cb = conv_b.reshape(1, -1)
    return pl.pallas_call(
        functools.partial(_ffn_kernel, tiles_per_seq=seq // ROW_TILE, d_ff=d_ff),
        grid=(t // ROW_TILE,),
        in_specs=[_row_spec(d), _row_spec(d), _resident(w_up.shape), _resident(conv_w.shape), _resident(cb.shape),
                  _resident(w_down.shape), _resident((1, d)), _resident((1, d))],
        out_specs=[_row_spec(d), _row_spec(d)],
        out_shape=[jax.ShapeDtypeStruct((t, d), F32), jax.ShapeDtypeStruct((t, d), BF16)],
        scratch_shapes=[pltpu.VMEM((CONV_HALO, 2 * d_ff), F32),
                        pltpu.VMEM((4, ROW_TILE + CONV_HALO, COL_CHUNK), F32),
                        pltpu.VMEM((ROW_TILE, d), F32)],
        compiler_params=_params(("arbitrary",)),
        name=name,
    )(xb, x32, w_up, conv_w, cb, w_down, g.reshape(1, d), b.reshape(1, d))


def _ssd_kernel(xs_ref, bm_ref, cm_ref, dt_ref, z_ref, alog_ref, dskip_ref, ng_ref, o_ref, state):
    q = SSM_CHUNK
    gw = (SSM_HEADS // SSM_GROUPS) * SSM_HEADDIM
    pairs_per_group = gw // LANES

    @pl.when(pl.program_id(1) == 0)
    def _():
        state[...] = jnp.zeros(state.shape, F32)

    lane_row = lax.broadcasted_iota(jnp.int32, (1, LANES), 1)
    first_head = lane_row < SSM_HEADDIM
    dt = dt_ref[...]
    a_row = jnp.where(lane_row < SSM_HEADS, -jnp.exp(alog_ref[...]), 0.0)
    acs = _cumsum_rows(dt * a_row)
    acs_t = acs.T
    tot = acs[q - 1:q, :]
    row = lax.broadcasted_iota(jnp.int32, (q, q), 0)
    col = lax.broadcasted_iota(jnp.int32, (q, q), 1)
    causal = row >= col

    for g in range(SSM_GROUPS):
        bmat = bm_ref[:, g * SSM_STATE:(g + 1) * SSM_STATE]
        cmat = cm_ref[:, g * SSM_STATE:(g + 1) * SSM_STATE]
        cb = lax.dot_general(cmat, bmat, (((1,), (1,)), ((), ())), preferred_element_type=F32)
        prev_t = state[g]
        y_off = _mm(cmat, prev_t.astype(BF16))
        xd_parts, dec_parts, u_parts = [], [], []
        for p in range(pairs_per_group):
            c0 = g * gw + p * LANES
            h0 = c0 // SSM_HEADDIM
            x2 = xs_ref[:, c0:c0 + LANES]
            acol = [acs[:, h0 + e:h0 + e + 1] for e in (0, 1)]
            dt2 = jnp.where(first_head, dt[:, h0:h0 + 1], dt[:, h0 + 1:h0 + 2])
            xdt2 = x2 * dt2
            xdt2_b = xdt2.astype(BF16)
            ydiag = []
            for e in (0, 1):
                seg = jnp.exp(jnp.where(causal, acol[e] - acs_t[h0 + e:h0 + e + 1, :], -jnp.inf))
                ydiag.append(_mm((cb * seg).astype(BF16), xdt2_b))
            grow2 = jnp.where(first_head, jnp.exp(acol[0]), jnp.exp(acol[1]))
            y2 = jnp.where(first_head, ydiag[0], ydiag[1]) + y_off[:, p * LANES:(p + 1) * LANES] * grow2
            dend2 = jnp.where(first_head,
                              jnp.exp(tot[:, h0:h0 + 1] - acol[0]),
                              jnp.exp(tot[:, h0 + 1:h0 + 2] - acol[1]))
            xd_parts.append((xdt2 * dend2).astype(BF16))
            dec_parts.append(jnp.where(first_head, jnp.exp(tot[:, h0:h0 + 1]), jnp.exp(tot[:, h0 + 1:h0 + 2])))
            y2 = y2 + dskip_ref[:, c0:c0 + LANES] * x2
            zz = z_ref[:, c0:c0 + LANES]
            u_parts.append(y2 * (zz * _sigmoid(zz)))
        xd = jnp.concatenate(xd_parts, axis=1)
        new_t = lax.dot_general(bmat, xd, (((0,), (0,)), ((), ())), preferred_element_type=F32)
        state[g] = prev_t * jnp.concatenate(dec_parts, axis=1) + new_t
        u = jnp.concatenate(u_parts, axis=1)
        ms = jnp.mean(u * u, axis=-1, keepdims=True)
        o_ref[:, g * gw:(g + 1) * gw] = (u * lax.rsqrt(ms + LN_EPS) * ng_ref[:, g * gw:(g + 1) * gw]).astype(o_ref.dtype)


def _ssd_call(xs, bm, cm, dt, z, alog_row, dskip_row, ng_row, *, batch, seq):
    t, inner = xs.shape
    nc = seq // SSM_CHUNK
    q = SSM_CHUNK
    gw = inner // SSM_GROUPS
    rows = lambda b, c: (b * nc + c, 0)
    const = lambda b, c: (0, 0)
    return pl.pallas_call(
        _ssd_kernel,
        grid=(batch, nc),
        in_specs=[pl.BlockSpec((q, inner), rows),
                  pl.BlockSpec((q, bm.shape[1]), rows),
                  pl.BlockSpec((q, cm.shape[1]), rows),
                  pl.BlockSpec((q, LANES), rows),
                  pl.BlockSpec((q, inner), rows),
                  pl.BlockSpec((1, LANES), const),
                  pl.BlockSpec((1, inner), const),
                  pl.BlockSpec((1, inner), const)],
        out_specs=pl.BlockSpec((q, inner), rows),
        out_shape=jax.ShapeDtypeStruct((t, inner), BF16),
        scratch_shapes=[pltpu.VMEM((SSM_GROUPS, SSM_STATE, gw), F32)],
        compiler_params=_params(("arbitrary", "arbitrary")),
        name="ssd_scan",
    )(xs, bm, cm, dt, z, alog_row, dskip_row, ng_row)


def _split3(x):
    hi = x.astype(BF16)
    r1 = x - hi.astype(F32)
    mid = r1.astype(BF16)
    lo = (r1 - mid.astype(F32)).astype(BF16)
    return hi, mid, lo


def _l1_in_kernel(x_ref, w_ref, wf_ref, bf_ref, qkv_ref, eq_ref, ek_ref, carry, *, tiles_per_seq):
    seq_start = pl.program_id(0) % tiles_per_seq == 0
    x = x_ref[...]
    for cols in _chunks(qkv_ref.shape[1]):
        qkv_ref[:, cols] = _mm(x, w_ref[:, cols]).astype(qkv_ref.dtype)
    f = _mm(x, wf_ref[...]) + bf_ref[...]
    cum = _cumsum_rows(-_softplus(-f)) + jnp.where(seq_start, 0.0, carry[...])
    carry[...] = cum[cum.shape[0] - 1:, :]
    parts = _split3(cum)
    r = lax.broadcasted_iota(jnp.int32, (LANES, LANES), 0)
    l = lax.broadcasted_iota(jnp.int32, (LANES, LANES), 1)
    lane = lax.broadcasted_iota(jnp.int32, (1, LANES), 1) & (FOX_EXT - 1)
    spread_q = spread_k = None
    for c, part in enumerate(parts):
        sq = _mm(part, jnp.where(l == FOX_EXT * r + c, 1.0, 0.0).astype(BF16))
        sk = _mm(part, jnp.where(l == FOX_EXT * r + 3 + c, 1.0, 0.0).astype(BF16))
        spread_q = sq if spread_q is None else spread_q + sq
        spread_k = sk if spread_k is None else spread_k + sk
    ones_q = jnp.where((lane >= 3) & (lane < 6), 1.0, 0.0)
    ones_k = jnp.where(lane < 3, 1.0, 0.0)
    eq_ref[...] = (spread_q + ones_q).astype(BF16)
    ek_ref[...] = (ones_k - spread_k).astype(BF16)


def _l1_in_call(xb, w_qkv, w_f, b_f, *, seq):
    t, d = xb.shape
    n = w_qkv.shape[1]
    return pl.pallas_call(
        functools.partial(_l1_in_kernel, tiles_per_seq=seq // ROW_TILE),
        grid=(t // ROW_TILE,),
        in_specs=[_row_spec(d), _resident(w_qkv.shape), _resident(w_f.shape), _resident(b_f.shape)],
        out_specs=[_row_spec(n), _row_spec(LANES), _row_spec(LANES)],
        out_shape=[jax.ShapeDtypeStruct((t, n), BF16), jax.ShapeDtypeStruct((t, LANES), BF16),
                   jax.ShapeDtypeStruct((t, LANES), BF16)],
        scratch_shapes=[pltpu.VMEM((1, LANES), F32)],
        compiler_params=_params(("arbitrary",)),
        name="l1_in",
    )(xb, w_qkv, w_f, b_f)


def _fox_attn_kernel(q_ref, k_ref, v_ref, eq_ref, ek_ref, o_ref, vt_ref, *, tq, nq):
    hp = pl.program_id(1)
    lane2 = lax.broadcasted_iota(jnp.int32, (1, 2 * LANES), 1)
    head_lanes = []
    for e in (0, 1):
        ext0 = LANES + FOX_EXT * (2 * hp + e)
        head_lanes.append(((lane2 >= FOX_HEADDIM * e) & (lane2 < FOX_HEADDIM * (e + 1)))
                          | ((lane2 >= ext0) & (lane2 < ext0 + FOX_EXT)))
    key = lax.broadcasted_iota(jnp.int32, (tq, tq), 0)
    qry = lax.broadcasted_iota(jnp.int32, (tq, tq), 1)
    visible = key <= qry
    scale = FOX_HEADDIM ** -0.5

    for j in range(nq):
        vt_ref[j] = v_ref[j * tq:(j + 1) * tq, :].astype(F32).T.astype(BF16)

    def head_queries(i):
        r0 = pl.multiple_of(i * tq, tq)
        qe = jnp.concatenate([q_ref[pl.ds(r0, tq), :] * scale, eq_ref[pl.ds(r0, tq), :]], axis=1)
        return [jnp.where(m, qe, jnp.zeros_like(qe)) for m in head_lanes]

    def logits(qm, j):
        c0 = pl.multiple_of(j * tq, tq)
        ke = jnp.concatenate([k_ref[pl.ds(c0, tq), :], ek_ref[pl.ds(c0, tq), :]], axis=1)
        return tuple(lax.dot_general(ke, qm[e], (((1,), (1,)), ((), ())), preferred_element_type=F32)
                     for e in (0, 1))

    def softmax_pv(j, s, stats, diagonal):
        if diagonal:
            s = [jnp.where(visible, se, NEG_BIG) for se in s]
        m_new = [jnp.maximum(st[0], jnp.max(se, axis=0, keepdims=True)) for se, st in zip(s, stats)]
        alpha = [jnp.exp(st[0] - mn) for st, mn in zip(stats, m_new)]
        p = [jnp.exp(se - mn) for se, mn in zip(s, m_new)]
        l = [a * st[1] + jnp.sum(pe, axis=0, keepdims=True) for a, st, pe in zip(alpha, stats, p)]
        pv = [_mm(vt_ref[j, FOX_HEADDIM * e:FOX_HEADDIM * (e + 1), :], p[e].astype(BF16))
              for e in (0, 1)]
        acc = [a * st[2] + pve for a, st, pve in zip(alpha, stats, pv)]
        return tuple(zip(m_new, l, acc))

    def q_block(i, s_first):
        qm = head_queries(i)

        def kv_step(j, carry):
            s, stats = carry
            s_next = logits(qm, j + 1)
            return s_next, softmax_pv(j, s, stats, False)

        init = tuple((jnp.full((1, tq), NEG_BIG, F32), jnp.zeros((1, tq), F32), jnp.zeros((FOX_HEADDIM, tq), F32))
                     for _ in (0, 1))
        s, stats = lax.fori_loop(0, i, kv_step, (s_first, init))
        s_first_next = logits(head_queries(jnp.minimum(i + 1, nq - 1)), 0)
        stats = softmax_pv(i, s, stats, True)
        o_t = jnp.concatenate([stats[0][2] / stats[0][1], stats[1][2] / stats[1][1]], axis=0)
        o_ref[pl.ds(pl.multiple_of(i * tq, tq), tq), :] = o_t.T.astype(o_ref.dtype)
        return s_first_next

    lax.fori_loop(0, nq, q_block, logits(head_queries(0), 0))


def _fox_attn_call(qkv, eq, ek, *, batch, seq, tq):
    t = qkv.shape[0]
    width = FOX_HEADS * FOX_HEADDIM
    npairs = width // LANES
    blk = lambda off: pl.BlockSpec((seq, LANES), lambda b, hp, o=off: (b, o + hp))
    ext = pl.BlockSpec((seq, LANES), lambda b, hp: (b, 0))
    return pl.pallas_call(
        functools.partial(_fox_attn_kernel, tq=tq, nq=seq // tq),
        grid=(batch, npairs),
        in_specs=[blk(0), blk(npairs), blk(2 * npairs), ext, ext],
        out_specs=pl.BlockSpec((seq, LANES), lambda b, hp: (b, hp)),
        out_shape=jax.ShapeDtypeStruct((t, width), BF16),
        scratch_shapes=[pltpu.VMEM((seq // tq, LANES, tq), BF16)],
        compiler_params=_params(("arbitrary", "arbitrary")),
        name="fox_attn",
    )(qkv, qkv, qkv, eq, ek)


def _pad_cols(a, width):
    return jnp.pad(a, ((0, 0), (0, width - a.shape[1])))


def kernel(x, sc_ssm_w_in, sc_conv_w, ssm_conv_w, ssm_conv_b, ssm_dt_bias, ssm_a_log, ssm_d, ssm_norm_g,
           sc_ssm_w_out, fox_w_in, fox_b_f, fox_w_out, ffn_w_up, ffn_conv_w, ffn_conv_b, ffn_w_down,
           ln_mix_g, ln_mix_b, ln_ffn_g, ln_ffn_b):
    batch, seq, d = x.shape
    t = batch * seq
    x32 = x.reshape(t, d)
    xb = x32.astype(BF16)

    inner = SSM_HEADS * SSM_HEADDIM
    bc_w = SSM_GROUPS * SSM_STATE
    o_dt = 3 * d + 2 * inner + 2 * bc_w
    ya, z, xs, bm, cm, dt = _l0_in_call(
        xb, sc_ssm_w_in[0].astype(BF16), _pad_cols(sc_ssm_w_in[0][:, o_dt:], LANES).astype(BF16),
        sc_conv_w[0], ssm_conv_w[0], ssm_conv_b[0].reshape(1, -1), _pad_cols(ssm_dt_bias[0].reshape(1, -1), LANES),
        seq=seq, inner=inner, bc_w=bc_w)
    yb = _ssd_call(xs, bm, cm, dt, z,
                   _pad_cols(ssm_a_log[0].reshape(1, -1), LANES),
                   jnp.repeat(ssm_d[0], SSM_HEADDIM).reshape(1, -1),
                   ssm_norm_g[0].reshape(1, -1), batch=batch, seq=seq)
    w_out = sc_ssm_w_out[0].astype(BF16)
    x32, xb = _outproj_ln([ya, yb], [w_out[:d], w_out[d:]], x32, ln_mix_g[0], ln_mix_b[0], name="l0_out")
    x32, xb = _conv_ffn(x32, xb, ffn_w_up[0].astype(BF16), ffn_conv_w[0], ffn_conv_b[0],
                        ffn_w_down[0].astype(BF16), ln_ffn_g[0], ln_ffn_b[0], seq=seq, name="l0_ffn")

    width = FOX_HEADS * FOX_HEADDIM
    fw = fox_w_in[0]
    qkv, eq, ek = _l1_in_call(xb, fw[:, :3 * width].astype(BF16), _pad_cols(fw[:, 3 * width:], LANES).astype(BF16),
                              _pad_cols(fox_b_f[0].reshape(1, -1), LANES), seq=seq)
    o = _fox_attn_call(qkv, eq, ek, batch=batch, seq=seq, tq=256)
    x32, xb = _outproj_ln([o], [fox_w_out[0].astype(BF16)], x32, ln_mix_g[1], ln_mix_b[1], name="l1_out")
    x32, xb = _conv_ffn(x32, xb, ffn_w_up[1].astype(BF16), ffn_conv_w[1], ffn_conv_b[1],
                        ffn_w_down[1].astype(BF16), ln_ffn_g[1], ln_ffn_b[1], seq=seq, name="l1_ffn")
    return x32.reshape(batch, seq, d)
```

```python
import functools

import jax
import jax.numpy as jnp
from jax import lax
from jax.experimental import pallas as pl
from jax.experimental.pallas import tpu as pltpu

F32 = jnp.float32
BF16 = jnp.bfloat16

DEPTH = 2
ALPHA = (2 * DEPTH) ** 0.25
LN_EPS = 1e-5

SSM_HEADDIM = 64
SSM_HEADS = 16
SSM_GROUPS = 2
SSM_STATE = 128
SSM_CHUNK = 128

FOX_HEADS = 16
FOX_HEADDIM = 64
FOX_EXT = 8
FOX_ONES = 16
LOG2E = 1.4426950408889634

LANES = 128
ROW_TILE = 512
COL_CHUNK = 256
CONV_HALO = 8
NEG_BIG = -1e30
VMEM_LIMIT = 56 * 1024 * 1024


def _params(sem):
    return pltpu.CompilerParams(dimension_semantics=sem, vmem_limit_bytes=VMEM_LIMIT)


def _mm(a, b):
    return jnp.dot(a, b, preferred_element_type=F32)


def _sigmoid(x):
    return 1.0 / (1.0 + jnp.exp(-x))


def _softplus(x):
    return jnp.maximum(x, 0.0) + jnp.log1p(jnp.exp(-jnp.abs(x)))


def _cumsum_rows(x):
    n = x.shape[0]
    row = lax.broadcasted_iota(jnp.int32, x.shape, 0)
    s = 1
    while s < n:
        x = x + jnp.where(row >= s, pltpu.roll(x, s, 0), 0.0)
        s *= 2
    return x


def _chunks(width):
    assert width % COL_CHUNK == 0
    return [slice(c, c + COL_CHUNK) for c in range(0, width, COL_CHUNK)]


def _shift(cols, off):
    return slice(cols.start + off, cols.stop + off)


def _causal_conv(val, halo, cbuf, slot, cols, w, taps, seq_start):
    tm = val.shape[0]
    cbuf[slot, 0:CONV_HALO, :] = jnp.where(seq_start, 0.0, halo[:, cols])
    cbuf[slot, CONV_HALO:CONV_HALO + tm, :] = val
    halo[:, cols] = val[tm - CONV_HALO:tm, :]
    out = val * w[taps - 1:taps, :]
    for k in range(taps - 1):
        sh = taps - 1 - k
        out = out + cbuf[slot, CONV_HALO - sh:CONV_HALO - sh + tm, :] * w[k:k + 1, :]
    return out


def _layer_norm(y, g, b):
    mu = jnp.mean(y, axis=-1, keepdims=True)
    d = y - mu
    var = jnp.mean(d * d, axis=-1, keepdims=True)
    return d * lax.rsqrt(var + LN_EPS) * g + b


def _resident(shape):
    return pl.BlockSpec(shape, lambda i: (0,) * len(shape), pipeline_mode=pl.Buffered(1))


def _row_spec(cols):
    return pl.BlockSpec((ROW_TILE, cols), lambda i: (i, 0))


def _l0_in_kernel(x_ref, w_ref, wdt_ref, scw_ref, xcw_ref, xcb_ref, dtb_ref,
                  ya_ref, z_ref, xs_ref, bm_ref, cm_ref, dt_ref,
                  halo_sc, halo_x, cbuf, *, tiles_per_seq, d, inner, bc_w):
    seq_start = pl.program_id(0) % tiles_per_seq == 0
    x = x_ref[...]
    for n, cols in enumerate(_chunks(d)):
        gate = _mm(x, w_ref[:, cols])
        ch = _mm(x, w_ref[:, _shift(cols, d)]) * _mm(x, w_ref[:, _shift(cols, 2 * d)])
        y = _causal_conv(ch, halo_sc, cbuf, n % 2, cols, scw_ref[:, cols], scw_ref.shape[0], seq_start)
        ya_ref[:, cols] = (gate * y).astype(ya_ref.dtype)
    for cols in _chunks(inner):
        z_ref[:, cols] = _mm(x, w_ref[:, _shift(cols, 3 * d)]).astype(z_ref.dtype)
    xbc0 = 3 * d + inner
    for n, cols in enumerate(_chunks(inner + 2 * bc_w)):
        v = _mm(x, w_ref[:, _shift(cols, xbc0)])
        y = _causal_conv(v, halo_x, cbuf, n % 2, cols, xcw_ref[:, cols], xcw_ref.shape[0], seq_start) + xcb_ref[:, cols]
        y = y * _sigmoid(y)
        if cols.stop <= inner:
            xs_ref[:, cols] = y.astype(xs_ref.dtype)
        elif cols.stop <= inner + bc_w:
            bm_ref[:, _shift(cols, -inner)] = y.astype(bm_ref.dtype)
        else:
            cm_ref[:, _shift(cols, -inner - bc_w)] = y.astype(cm_ref.dtype)
    dt_ref[...] = _softplus(_mm(x, wdt_ref[...]) + dtb_ref[...])


def _l0_in_call(xb, w_in, w_dt, sc_cw, x_cw, x_cb, dt_b, *, seq, inner, bc_w):
    t, d = xb.shape
    assert bc_w == COL_CHUNK
    kern = functools.partial(_l0_in_kernel, tiles_per_seq=seq // ROW_TILE, d=d, inner=inner, bc_w=bc_w)
    return pl.pallas_call(
        kern,
        grid=(t // ROW_TILE,),
        in_specs=[_row_spec(d), _resident(w_in.shape), _resident(w_dt.shape), _resident(sc_cw.shape),
                  _resident(x_cw.shape), _resident(x_cb.shape), _resident(dt_b.shape)],
        out_specs=[_row_spec(d), _row_spec(inner), _row_spec(inner), _row_spec(bc_w), _row_spec(bc_w), _row_spec(LANES)],
        out_shape=[jax.ShapeDtypeStruct((t, d), BF16), jax.ShapeDtypeStruct((t, inner), F32),
                   jax.ShapeDtypeStruct((t, inner), F32), jax.ShapeDtypeStruct((t, bc_w), BF16),
                   jax.ShapeDtypeStruct((t, bc_w), BF16), jax.ShapeDtypeStruct((t, LANES), F32)],
        scratch_shapes=[pltpu.VMEM((CONV_HALO, d), F32), pltpu.VMEM((CONV_HALO, inner + 2 * bc_w), F32),
                        pltpu.VMEM((2, ROW_TILE + CONV_HALO, COL_CHUNK), F32)],
        compiler_params=_params(("arbitrary",)),
        name="l0_in",
    )(xb, w_in, w_dt, sc_cw, x_cw, x_cb, dt_b)


def _outproj_ln_kernel(*refs, n_in):
    a_refs = refs[:n_in]
    w_refs = refs[n_in:2 * n_in]
    xres_ref, g_ref, b_ref, o32_ref, o16_ref = refs[2 * n_in:]
    acc = _mm(a_refs[0][...], w_refs[0][...])
    for a_ref, w_ref in zip(a_refs[1:], w_refs[1:]):
        acc = acc + _mm(a_ref[...], w_ref[...])
    out = _layer_norm(ALPHA * xres_ref[...] + acc, g_ref[...], b_ref[...])
    o32_ref[...] = out
    o16_ref[...] = out.astype(BF16)


def _outproj_ln(acts, weights, xres, g, b, *, name):
    t, d = xres.shape
    in_specs = [_row_spec(a.shape[1]) for a in acts]
    in_specs += [_resident(w.shape) for w in weights]
    in_specs += [_row_spec(d), _resident((1, d)), _resident((1, d))]
    return pl.pallas_call(
        functools.partial(_outproj_ln_kernel, n_in=len(acts)),
        grid=(t // ROW_TILE,),
        in_specs=in_specs,
        out_specs=[_row_spec(d), _row_spec(d)],
        out_shape=[jax.ShapeDtypeStruct((t, d), F32), jax.ShapeDtypeStruct((t, d), BF16)],
        compiler_params=_params(("arbitrary",)),
        name=name,
    )(*acts, *weights, xres, g.reshape(1, d), b.reshape(1, d))


def _ffn_kernel(x_ref, xres_ref, wup_ref, cw_ref, cb_ref, wd_ref, g_ref, b_ref, o32_ref, o16_ref,
                halo, cbuf, act_ref, *, tiles_per_seq, d_ff):
    seq_start = pl.program_id(0) % tiles_per_seq == 0
    taps = cw_ref.shape[0]
    chunks = _chunks(d_ff)

    def up(cols):
        return _mm(x_ref[...], wup_ref[:, cols]), _mm(x_ref[...], wup_ref[:, _shift(cols, d_ff)])

    ahead = up(chunks[0])
    for n, cols in enumerate(chunks):
        gcols = _shift(cols, d_ff)
        u_raw, g_raw = ahead
        if n + 1 < len(chunks):
            ahead = up(chunks[n + 1])
        u = _causal_conv(u_raw, halo, cbuf, 2 * (n % 2), cols, cw_ref[:, cols], taps, seq_start) + cb_ref[:, cols]
        gt = _causal_conv(g_raw, halo, cbuf, 2 * (n % 2) + 1, gcols, cw_ref[:, gcols], taps, seq_start) + cb_ref[:, gcols]
        act_ref[:, cols] = (u * (gt * _sigmoid(gt))).astype(BF16)
    out = _layer_norm(ALPHA * xres_ref[...] + _mm(act_ref[...], wd_ref[...]), g_ref[...], b_ref[...])
    o32_ref[...] = out
    o16_ref[...] = out.astype(BF16)


def _conv_ffn(x32, xb, w_up, conv_w, conv_b, w_down, g, b, *, seq, name):
    t, d = x32.shape
    d_ff = w_down.shape[0]
    cb = conv_b.reshape(1, -1)
    return pl.pallas_call(
        functools.partial(_ffn_kernel, tiles_per_seq=seq // ROW_TILE, d_ff=d_ff),
        grid=(t // ROW_TILE,),
        in_specs=[_row_spec(d), _row_spec(d), _resident(w_up.shape), _resident(conv_w.shape), _resident(cb.shape),
                  _resident(w_down.shape), _resident((1, d)), _resident((1, d))],
        out_specs=[_row_spec(d), _row_spec(d)],
        out_shape=[jax.ShapeDtypeStruct((t, d), F32), jax.ShapeDtypeStruct((t, d), BF16)],
        scratch_shapes=[pltpu.VMEM((CONV_HALO, 2 * d_ff), F32),
                        pltpu.VMEM((4, ROW_TILE + CONV_HALO, COL_CHUNK), F32),
                        pltpu.VMEM((ROW_TILE, d_ff), BF16)],
        compiler_params=_params(("arbitrary",)),
        name=name,
    )(xb, x32, w_up, conv_w, cb, w_down, g.reshape(1, d), b.reshape(1, d))


def _ssd_kernel(xs_ref, bm_ref, cm_ref, dt_ref, z_ref, alog_ref, dskip_ref, ng_ref, o_ref, state):
    q = SSM_CHUNK
    gw = (SSM_HEADS // SSM_GROUPS) * SSM_HEADDIM
    pairs_per_group = gw // LANES

    @pl.when(pl.program_id(1) == 0)
    def _():
        state[...] = jnp.zeros(state.shape, F32)

    lane_row = lax.broadcasted_iota(jnp.int32, (1, LANES), 1)
    first_head = lane_row < SSM_HEADDIM
    dt = dt_ref[...]
    a_row = jnp.where(lane_row < SSM_HEADS, -jnp.exp(alog_ref[...]), 0.0)
    acs = _cumsum_rows(dt * a_row)
    acs_t = acs.T
    tot = acs[q - 1:q, :]
    row = lax.broadcasted_iota(jnp.int32, (q, q), 0)
    col = lax.broadcasted_iota(jnp.int32, (q, q), 1)
    causal = row >= col

    for g in range(SSM_GROUPS):
        bmat = bm_ref[:, g * SSM_STATE:(g + 1) * SSM_STATE]
        cmat = cm_ref[:, g * SSM_STATE:(g + 1) * SSM_STATE]
        cb = lax.dot_general(cmat, bmat, (((1,), (1,)), ((), ())), preferred_element_type=F32)
        prev_t = state[g]
        y_off = _mm(cmat, prev_t.astype(BF16))
        xd_parts, dec_parts, u_parts = [], [], []
        for p in range(pairs_per_group):
            c0 = g * gw + p * LANES
            h0 = c0 // SSM_HEADDIM
            x2 = xs_ref[:, c0:c0 + LANES]
            acol = [acs[:, h0 + e:h0 + e + 1] for e in (0, 1)]
            dt2 = jnp.where(first_head, dt[:, h0:h0 + 1], dt[:, h0 + 1:h0 + 2])
            xdt2 = x2 * dt2
            xdt2_b = xdt2.astype(BF16)
            ydiag = []
            for e in (0, 1):
                seg = jnp.exp(jnp.where(causal, acol[e] - acs_t[h0 + e:h0 + e + 1, :], -jnp.inf))
                ydiag.append(_mm((cb * seg).astype(BF16), xdt2_b))
            grow2 = jnp.where(first_head, jnp.exp(acol[0]), jnp.exp(acol[1]))
            y2 = jnp.where(first_head, ydiag[0], ydiag[1]) + y_off[:, p * LANES:(p + 1) * LANES] * grow2
            dend2 = jnp.where(first_head,
                              jnp.exp(tot[:, h0:h0 + 1] - acol[0]),
                              jnp.exp(tot[:, h0 + 1:h0 + 2] - acol[1]))
            xd_parts.append((xdt2 * dend2).astype(BF16))
            dec_parts.append(jnp.where(first_head, jnp.exp(tot[:, h0:h0 + 1]), jnp.exp(tot[:, h0 + 1:h0 + 2])))
            y2 = y2 + dskip_ref[:, c0:c0 + LANES] * x2
            zz = z_ref[:, c0:c0 + LANES]
            u_parts.append(y2 * (zz * _sigmoid(zz)))
        xd = jnp.concatenate(xd_parts, axis=1)
        new_t = lax.dot_general(bmat, xd, (((0,), (0,)), ((), ())), preferred_element_type=F32)
        state[g] = prev_t * jnp.concatenate(dec_parts, axis=1) + new_t
        u = jnp.concatenate(u_parts, axis=1)
        ms = jnp.mean(u * u, axis=-1, keepdims=True)
        o_ref[:, g * gw:(g + 1) * gw] = (u * lax.rsqrt(ms + LN_EPS) * ng_ref[:, g * gw:(g + 1) * gw]).astype(o_ref.dtype)


def _ssd_call(xs, bm, cm, dt, z, alog_row, dskip_row, ng_row, *, batch, seq):
    t, inner = xs.shape
    nc = seq // SSM_CHUNK
    q = SSM_CHUNK
    gw = inner // SSM_GROUPS
    rows = lambda b, c: (b * nc + c, 0)
    const = lambda b, c: (0, 0)
    return pl.pallas_call(
        _ssd_kernel,
        grid=(batch, nc),
        in_specs=[pl.BlockSpec((q, inner), rows),
                  pl.BlockSpec((q, bm.shape[1]), rows),
                  pl.BlockSpec((q, cm.shape[1]), rows),
                  pl.BlockSpec((q, LANES), rows),
                  pl.BlockSpec((q, inner), rows),
                  pl.BlockSpec((1, LANES), const),
                  pl.BlockSpec((1, inner), const),
                  pl.BlockSpec((1, inner), const)],
        out_specs=pl.BlockSpec((q, inner), rows),
        out_shape=jax.ShapeDtypeStruct((t, inner), BF16),
        scratch_shapes=[pltpu.VMEM((SSM_GROUPS, SSM_STATE, gw), F32)],
        compiler_params=_params(("arbitrary", "arbitrary")),
        name="ssd_scan",
    )(xs, bm, cm, dt, z, alog_row, dskip_row, ng_row)


def _split3(x):
    hi = x.astype(BF16)
    r1 = x - hi.astype(F32)
    mid = r1.astype(BF16)
    lo = (r1 - mid.astype(F32)).astype(BF16)
    return hi, mid, lo


def _l1_in_kernel(x_ref, w_ref, wf_ref, bf_ref, qkv_ref, eq_ref, ek_ref, carry, *, tiles_per_seq, q_width):
    seq_start = pl.program_id(0) % tiles_per_seq == 0
    x = x_ref[...]
    for cols in _chunks(qkv_ref.shape[1]):
        y = _mm(x, w_ref[:, cols])
        if cols.stop <= q_width:
            y = y * (FOX_HEADDIM ** -0.5 * LOG2E)
        qkv_ref[:, cols] = y.astype(qkv_ref.dtype)
    f = _mm(x, wf_ref[...]) + bf_ref[...]
    cum = _cumsum_rows(-_softplus(-f)) + jnp.where(seq_start, 0.0, carry[...])
    carry[...] = cum[cum.shape[0] - 1:, :]
    parts = _split3(cum * LOG2E)
    r = lax.broadcasted_iota(jnp.int32, (LANES, LANES), 0)
    l = lax.broadcasted_iota(jnp.int32, (LANES, LANES), 1)
    lane = lax.broadcasted_iota(jnp.int32, (1, LANES), 1) & (FOX_EXT - 1)
    spread_q = spread_k = None
    for c, part in enumerate(parts):
        sq = _mm(part, jnp.where(l == FOX_EXT * r + c, 1.0, 0.0).astype(BF16))
        sk = _mm(part, jnp.where(l == FOX_EXT * r + 3 + c, 1.0, 0.0).astype(BF16))
        spread_q = sq if spread_q is None else spread_q + sq
        spread_k = sk if spread_k is None else spread_k + sk
    ones_q = jnp.where((lane >= 3) & (lane < 6), 1.0, 0.0)
    ones_k = jnp.where(lane < 3, 1.0, 0.0)
    eq_ref[...] = (spread_q + ones_q).astype(BF16)
    ek_ref[...] = (ones_k - spread_k).astype(BF16)


def _l1_in_call(xb, w_qkv, w_f, b_f, *, seq):
    t, d = xb.shape
    n = w_qkv.shape[1]
    return pl.pallas_call(
        functools.partial(_l1_in_kernel, tiles_per_seq=seq // ROW_TILE, q_width=n // 3),
        grid=(t // ROW_TILE,),
        in_specs=[_row_spec(d), _resident(w_qkv.shape), _resident(w_f.shape), _resident(b_f.shape)],
        out_specs=[_row_spec(n), _row_spec(LANES), _row_spec(LANES)],
        out_shape=[jax.ShapeDtypeStruct((t, n), BF16), jax.ShapeDtypeStruct((t, LANES), BF16),
                   jax.ShapeDtypeStruct((t, LANES), BF16)],
        scratch_shapes=[pltpu.VMEM((1, LANES), F32)],
        compiler_params=_params(("arbitrary",)),
        name="l1_in",
    )(xb, w_qkv, w_f, b_f)


def _fox_attn_kernel(q_ref, k_ref, v_ref, eq_ref, ek_ref, o_ref, vt_ref, *, tq, nq):
    hp = pl.program_id(1)
    lane2 = lax.broadcasted_iota(jnp.int32, (1, 2 * LANES), 1)
    head_lanes = []
    for e in (0, 1):
        ext0 = LANES + FOX_EXT * (2 * hp + e)
        head_lanes.append(((lane2 >= FOX_HEADDIM * e) & (lane2 < FOX_HEADDIM * (e + 1)))
                          | ((lane2 >= ext0) & (lane2 < ext0 + FOX_EXT)))
    key = lax.broadcasted_iota(jnp.int32, (tq, tq), 0)
    qry = lax.broadcasted_iota(jnp.int32, (tq, tq), 1)
    visible = key <= qry

    vrows = FOX_HEADDIM + FOX_ONES
    for j in range(nq):
        v_t = v_ref[j * tq:(j + 1) * tq, :].astype(F32).T.astype(BF16)
        for e in (0, 1):
            vt_ref[j, e, 0:FOX_HEADDIM, :] = v_t[FOX_HEADDIM * e:FOX_HEADDIM * (e + 1), :]
            vt_ref[j, e, FOX_HEADDIM:vrows, :] = jnp.ones((FOX_ONES, tq), BF16)

    def head_queries(i):
        rows = slice(i * tq, (i + 1) * tq)
        qe = jnp.concatenate([q_ref[rows, :], eq_ref[rows, :]], axis=1)
        return [jnp.where(m, qe, jnp.zeros_like(qe)) for m in head_lanes]

    def logits(qm, j):
        rows = slice(j * tq, (j + 1) * tq)
        ke = jnp.concatenate([k_ref[rows, :], ek_ref[rows, :]], axis=1)
        return tuple(lax.dot_general(ke, qm[e], (((1,), (1,)), ((), ())), preferred_element_type=F32)
                     for e in (0, 1))

    def softmax(s, m, diagonal):
        if diagonal:
            s = [jnp.where(visible, se, NEG_BIG) for se in s]
        m_new = tuple(jnp.maximum(me, jnp.max(se, axis=0, keepdims=True)) for se, me in zip(s, m))
        alpha = tuple(jnp.exp2(me - mn) for me, mn in zip(m, m_new))
        p = tuple(jnp.exp2(se - mn).astype(BF16) for se, mn in zip(s, m_new))
        return m_new, alpha, p

    def weighted_values(j, p):
        return [_mm(vt_ref[j, e], p[e]) for e in (0, 1)]

    steps = [(i, j) for i in range(nq) for j in range(i + 1)]
    queries = {}

    def masked_queries(i):
        if i not in queries:
            queries[i] = head_queries(i)
        return queries[i]

    def finish(i, acc):
        o_t = jnp.concatenate([ac[0:FOX_HEADDIM] / ac[FOX_HEADDIM:FOX_HEADDIM + 1] for ac in acc], axis=0)
        o_ref[i * tq:(i + 1) * tq, :] = o_t.T.astype(o_ref.dtype)

    def retire(pending, accs):
        (pi, pj, p, alpha), pv = pending
        accs[pi] = pv if pj == 0 else [a * ac + pve for a, ac, pve in zip(alpha, accs[pi], pv)]
        if pj == pi:
            finish(pi, accs.pop(pi))

    accs = {}
    pending = None
    s_next = logits(masked_queries(0), 0)
    m = None
    for k, (i, j) in enumerate(steps):
        s = s_next
        if k + 1 < len(steps):
            s_next = logits(masked_queries(steps[k + 1][0]), steps[k + 1][1])
        if pending is not None:
            pending = (pending, weighted_values(pending[1], pending[2]))
        if j == 0:
            m = tuple(jnp.full((1, tq), NEG_BIG, F32) for _ in (0, 1))
        m, alpha, p = softmax(s, m, j == i)
        if pending is not None:
            retire(pending, accs)
        pending = (i, j, p, alpha)
    retire((pending, weighted_values(pending[1], pending[2])), accs)


def _fox_attn_call(qkv, eq, ek, *, batch, seq, tq):
    t = qkv.shape[0]
    width = FOX_HEADS * FOX_HEADDIM
    npairs = width // LANES
    blk = lambda off: pl.BlockSpec((seq, LANES), lambda b, hp, o=off: (b, o + hp))
    ext = pl.BlockSpec((seq, LANES), lambda b, hp: (b, 0))
    return pl.pallas_call(
        functools.partial(_fox_attn_kernel, tq=tq, nq=seq // tq),
        grid=(batch, npairs),
        in_specs=[blk(0), blk(npairs), blk(2 * npairs), ext, ext],
        out_specs=pl.BlockSpec((seq, LANES), lambda b, hp: (b, hp)),
        out_shape=jax.ShapeDtypeStruct((t, width), BF16),
        scratch_shapes=[pltpu.VMEM((seq // tq, 2, FOX_HEADDIM + FOX_ONES, tq), BF16)],
        compiler_params=_params(("arbitrary", "arbitrary")),
        name="fox_attn",
    )(qkv, qkv, qkv, eq, ek)


def _pad_cols(a, width):
    return jnp.pad(a, ((0, 0), (0, width - a.shape[1])))


def kernel(x, sc_ssm_w_in, sc_conv_w, ssm_conv_w, ssm_conv_b, ssm_dt_bias, ssm_a_log, ssm_d, ssm_norm_g,
           sc_ssm_w_out, fox_w_in, fox_b_f, fox_w_out, ffn_w_up, ffn_conv_w, ffn_conv_b, ffn_w_down,
           ln_mix_g, ln_mix_b, ln_ffn_g, ln_ffn_b):
    batch, seq, d = x.shape
    t = batch * seq
    x32 = x.reshape(t, d)
    xb = x32.astype(BF16)

    inner = SSM_HEADS * SSM_HEADDIM
    bc_w = SSM_GROUPS * SSM_STATE
    o_dt = 3 * d + 2 * inner + 2 * bc_w
    ya, z, xs, bm, cm, dt = _l0_in_call(
        xb, sc_ssm_w_in[0].astype(BF16), _pad_cols(sc_ssm_w_in[0][:, o_dt:], LANES).astype(BF16),
        sc_conv_w[0], ssm_conv_w[0], ssm_conv_b[0].reshape(1, -1), _pad_cols(ssm_dt_bias[0].reshape(1, -1), LANES),
        seq=seq, inner=inner, bc_w=bc_w)
    yb = _ssd_call(xs, bm, cm, dt, z,
                   _pad_cols(ssm_a_log[0].reshape(1, -1), LANES),
                   jnp.repeat(ssm_d[0], SSM_HEADDIM).reshape(1, -1),
                   ssm_norm_g[0].reshape(1, -1), batch=batch, seq=seq)
    w_out = sc_ssm_w_out[0].astype(BF16)
    x32, xb = _outproj_ln([ya, yb], [w_out[:d], w_out[d:]], x32, ln_mix_g[0], ln_mix_b[0], name="l0_out")
    x32, xb = _conv_ffn(x32, xb, ffn_w_up[0].astype(BF16), ffn_conv_w[0], ffn_conv_b[0],
                        ffn_w_down[0].astype(BF16), ln_ffn_g[0], ln_ffn_b[0], seq=seq, name="l0_ffn")

    width = FOX_HEADS * FOX_HEADDIM
    fw = fox_w_in[0]
    qkv, eq, ek = _l1_in_call(xb, fw[:, :3 * width].astype(BF16), _pad_cols(fw[:, 3 * width:], LANES).astype(BF16),
                              _pad_cols(fox_b_f[0].reshape(1, -1), LANES), seq=seq)
    o = _fox_attn_call(qkv, eq, ek, batch=batch, seq=seq, tq=256)
    x32, xb = _outproj_ln([o], [fox_w_out[0].astype(BF16)], x32, ln_mix_g[1], ln_mix_b[1], name="l1_out")
    x32, xb = _conv_ffn(x32, xb, ffn_w_up[1].astype(BF16), ffn_conv_w[1], ffn_conv_b[1],
                        ffn_w_down[1].astype(BF16), ln_ffn_g[1], ln_ffn_b[1], seq=seq, name="l1_ffn")
    return x32.reshape(batch, seq, d)
```

```python
import functools

import jax
import jax.numpy as jnp
from jax import lax
from jax.experimental import pallas as pl
from jax.experimental.pallas import tpu as pltpu

F32 = jnp.float32
BF16 = jnp.bfloat16

DEPTH = 2
ALPHA = (2 * DEPTH) ** 0.25
LN_EPS = 1e-5

SSM_HEADDIM = 64
SSM_HEADS = 16
SSM_GROUPS = 2
SSM_STATE = 128
SSM_CHUNK = 128

FOX_HEADS = 16
FOX_HEADDIM = 64
FOX_EXT = 8
FOX_ONES = 16
LOG2E = 1.4426950408889634

LANES = 128
ROW_TILE = 512
COL_CHUNK = 256
CONV_HALO = 8
NEG_BIG = -1e30
VMEM_LIMIT = 56 * 1024 * 1024


def _params(sem):
    return pltpu.CompilerParams(dimension_semantics=sem, vmem_limit_bytes=VMEM_LIMIT)


def _mm(a, b):
    return jnp.dot(a, b, preferred_element_type=F32)


def _sigmoid(x):
    return 1.0 / (1.0 + jnp.exp(-x))


def _softplus(x):
    return jnp.maximum(x, 0.0) + jnp.log1p(jnp.exp(-jnp.abs(x)))


def _cumsum_rows(x):
    n = x.shape[0]
    row = lax.broadcasted_iota(jnp.int32, x.shape, 0)
    s = 1
    while s < n:
        x = x + jnp.where(row >= s, pltpu.roll(x, s, 0), 0.0)
        s *= 2
    return x


def _chunks(width):
    assert width % COL_CHUNK == 0
    return [slice(c, c + COL_CHUNK) for c in range(0, width, COL_CHUNK)]


def _shift(cols, off):
    return slice(cols.start + off, cols.stop + off)


def _causal_conv(val, halo, cbuf, slot, cols, w, taps, seq_start):
    tm = val.shape[0]
    cbuf[slot, 0:CONV_HALO, :] = jnp.where(seq_start, 0.0, halo[:, cols])
    cbuf[slot, CONV_HALO:CONV_HALO + tm, :] = val
    halo[:, cols] = val[tm - CONV_HALO:tm, :]
    out = val * w[taps - 1:taps, :]
    for k in range(taps - 1):
        sh = taps - 1 - k
        out = out + cbuf[slot, CONV_HALO - sh:CONV_HALO - sh + tm, :] * w[k:k + 1, :]
    return out


def _issue_ahead(items):
    ahead = items[0][0]()
    for k, (_, finish) in enumerate(items):
        raw = ahead
        if k + 1 < len(items):
            ahead = items[k + 1][0]()
        finish(raw)


def _layer_norm(y, g, b):
    mu = jnp.mean(y, axis=-1, keepdims=True)
    d = y - mu
    var = jnp.mean(d * d, axis=-1, keepdims=True)
    return d * lax.rsqrt(var + LN_EPS) * g + b


def _resident(shape):
    return pl.BlockSpec(shape, lambda i: (0,) * len(shape), pipeline_mode=pl.Buffered(1))


def _resident_layer(shape, layer):
    return pl.BlockSpec((None,) + tuple(shape[1:]), lambda i: (layer,) + (0,) * (len(shape) - 1),
                        pipeline_mode=pl.Buffered(1))


def _row_spec(cols):
    return pl.BlockSpec((ROW_TILE, cols), lambda i: (i, 0))


def _l0_in_kernel(x_ref, w_ref, wdt_ref, scw_ref, xcw_ref, xcb_ref, dtb_ref,
                  ya_ref, z_ref, xs_ref, bm_ref, cm_ref, dt_ref,
                  halo_sc, halo_x, cbuf, xb_ref, *, tiles_per_seq, d, inner, bc_w):
    seq_start = pl.program_id(0) % tiles_per_seq == 0
    xb_ref[...] = x_ref[...].astype(BF16)

    def proj(off, cols):
        return _mm(xb_ref[...], w_ref[:, _shift(cols, off)])

    items = []
    for n, cols in enumerate(_chunks(d)):
        def sconv_done(raw, n=n, cols=cols):
            gate, c, h = raw
            y = _causal_conv(c * h, halo_sc, cbuf, n % 2, cols, scw_ref[:, cols], scw_ref.shape[0], seq_start)
            ya_ref[:, cols] = (gate * y).astype(ya_ref.dtype)
        items.append((lambda cols=cols: (proj(0, cols), proj(d, cols), proj(2 * d, cols)), sconv_done))
    for cols in _chunks(inner):
        def z_done(raw, cols=cols):
            z_ref[:, cols] = raw.astype(z_ref.dtype)
        items.append((lambda cols=cols: proj(3 * d, cols), z_done))
    for n, cols in enumerate(_chunks(inner + 2 * bc_w)):
        def xbc_done(raw, n=n, cols=cols):
            y = _causal_conv(raw, halo_x, cbuf, n % 2, cols, xcw_ref[:, cols], xcw_ref.shape[0], seq_start) + xcb_ref[:, cols]
            y = y * _sigmoid(y)
            if cols.stop <= inner:
                xs_ref[:, cols] = y.astype(xs_ref.dtype)
            elif cols.stop <= inner + bc_w:
                bm_ref[:, _shift(cols, -inner)] = y.astype(bm_ref.dtype)
            else:
                cm_ref[:, _shift(cols, -inner - bc_w)] = y.astype(cm_ref.dtype)
        items.append((lambda cols=cols: proj(3 * d + inner, cols), xbc_done))

    def dt_done(raw):
        dt_ref[...] = _softplus(raw + dtb_ref[...])
    items.append((lambda: _mm(xb_ref[...], wdt_ref[...]), dt_done))
    _issue_ahead(items)


def _l0_in_call(x32, w_in, w_dt, sc_cw, x_cw, x_cb, dt_b, *, seq, inner, bc_w):
    t, d = x32.shape
    assert bc_w == COL_CHUNK
    kern = functools.partial(_l0_in_kernel, tiles_per_seq=seq // ROW_TILE, d=d, inner=inner, bc_w=bc_w)
    return pl.pallas_call(
        kern,
        grid=(t // ROW_TILE,),
        in_specs=[_row_spec(d), _resident(w_in.shape), _resident(w_dt.shape), _resident(sc_cw.shape),
                  _resident(x_cw.shape), _resident(x_cb.shape), _resident(dt_b.shape)],
        out_specs=[_row_spec(d), _row_spec(inner), _row_spec(inner), _row_spec(bc_w), _row_spec(bc_w), _row_spec(LANES)],
        out_shape=[jax.ShapeDtypeStruct((t, d), BF16), jax.ShapeDtypeStruct((t, inner), F32),
                   jax.ShapeDtypeStruct((t, inner), F32), jax.ShapeDtypeStruct((t, bc_w), BF16),
                   jax.ShapeDtypeStruct((t, bc_w), BF16), jax.ShapeDtypeStruct((t, LANES), F32)],
        scratch_shapes=[pltpu.VMEM((CONV_HALO, d), F32), pltpu.VMEM((CONV_HALO, inner + 2 * bc_w), F32),
                        pltpu.VMEM((2, ROW_TILE + CONV_HALO, COL_CHUNK), F32),
                        pltpu.VMEM((ROW_TILE, d), BF16)],
        compiler_params=_params(("arbitrary",)),
        name="l0_in",
    )(x32, w_in, w_dt, sc_cw, x_cw, x_cb, dt_b)


def _outproj_ln_kernel(*refs, n_in):
    a_refs = refs[:n_in]
    w_refs = refs[n_in:2 * n_in]
    xres_ref, g_ref, b_ref, o32_ref, o16_ref = refs[2 * n_in:]
    acc = _mm(a_refs[0][...], w_refs[0][...])
    for a_ref, w_ref in zip(a_refs[1:], w_refs[1:]):
        acc = acc + _mm(a_ref[...], w_ref[...])
    out = _layer_norm(ALPHA * xres_ref[...] + acc, g_ref[...], b_ref[...])
    o32_ref[...] = out
    o16_ref[...] = out.astype(BF16)


def _outproj_ln(acts, w, xres, g, b, *, name):
    t, d = xres.shape
    width = acts[0].shape[1]
    assert all(a.shape[1] == width for a in acts) and w.shape[0] == width * len(acts)
    in_specs = [_row_spec(width) for _ in acts]
    in_specs += [pl.BlockSpec((width, d), lambda i, k=k: (k, 0), pipeline_mode=pl.Buffered(1)) for k in range(len(acts))]
    in_specs += [_row_spec(d), _resident((1, d)), _resident((1, d))]
    weights = [w] * len(acts)
    return pl.pallas_call(
        functools.partial(_outproj_ln_kernel, n_in=len(acts)),
        grid=(t // ROW_TILE,),
        in_specs=in_specs,
        out_specs=[_row_spec(d), _row_spec(d)],
        out_shape=[jax.ShapeDtypeStruct((t, d), F32), jax.ShapeDtypeStruct((t, d), BF16)],
        compiler_params=_params(("arbitrary",)),
        name=name,
    )(*acts, *weights, xres, g.reshape(1, d), b.reshape(1, d))


def _ffn_kernel(x_ref, xres_ref, wup_ref, cw_ref, cb_ref, wd_ref, g_ref, b_ref, o32_ref, o16_ref,
                halo, cbuf, act_ref, *, tiles_per_seq, d_ff):
    seq_start = pl.program_id(0) % tiles_per_seq == 0
    taps = cw_ref.shape[0]
    chunks = _chunks(d_ff)

    items = []
    for n, cols in enumerate(chunks):
        def gate(raw, n=n, cols=cols):
            gcols = _shift(cols, d_ff)
            u = _causal_conv(raw[0], halo, cbuf, 2 * (n % 2), cols, cw_ref[:, cols], taps, seq_start) + cb_ref[:, cols]
            gt = _causal_conv(raw[1], halo, cbuf, 2 * (n % 2) + 1, gcols, cw_ref[:, gcols], taps, seq_start) + cb_ref[:, gcols]
            act_ref[:, cols] = (u * (gt * _sigmoid(gt))).astype(BF16)
        items.append((lambda cols=cols: (_mm(x_ref[...], wup_ref[:, cols]),
                                         _mm(x_ref[...], wup_ref[:, _shift(cols, d_ff)])), gate))
    _issue_ahead(items)
    out = _layer_norm(ALPHA * xres_ref[...] + _mm(act_ref[...], wd_ref[...]), g_ref[...], b_ref[...])
    o32_ref[...] = out
    o16_ref[...] = out.astype(BF16)


def _conv_ffn(x32, xb, w_up, conv_w, conv_b, w_down, g, b, *, layer, seq, name):
    t, d = x32.shape
    d_ff = w_down.shape[1]
    cb = conv_b.reshape(1, -1)
    return pl.pallas_call(
        functools.partial(_ffn_kernel, tiles_per_seq=seq // ROW_TILE, d_ff=d_ff),
        grid=(t // ROW_TILE,),
        in_specs=[_row_spec(d), _row_spec(d), _resident_layer(w_up.shape, layer), _resident(conv_w.shape),
                  _resident(cb.shape), _resident_layer(w_down.shape, layer), _resident((1, d)), _resident((1, d))],
        out_specs=[_row_spec(d), _row_spec(d)],
        out_shape=[jax.ShapeDtypeStruct((t, d), F32), jax.ShapeDtypeStruct((t, d), BF16)],
        scratch_shapes=[pltpu.VMEM((CONV_HALO, 2 * d_ff), F32),
                        pltpu.VMEM((4, ROW_TILE + CONV_HALO, COL_CHUNK), F32),
                        pltpu.VMEM((ROW_TILE, d_ff), BF16)],
        compiler_params=_params(("arbitrary",)),
        name=name,
    )(xb, x32, w_up, conv_w, cb, w_down, g.reshape(1, d), b.reshape(1, d))


def _ssd_kernel(xs_ref, bm_ref, cm_ref, dt_ref, z_ref, alog_ref, dskip_ref, ng_ref, o_ref, state):
    q = SSM_CHUNK
    gw = (SSM_HEADS // SSM_GROUPS) * SSM_HEADDIM
    pairs_per_group = gw // LANES

    @pl.when(pl.program_id(1) == 0)
    def _():
        state[...] = jnp.zeros(state.shape, F32)

    lane_row = lax.broadcasted_iota(jnp.int32, (1, LANES), 1)
    first_head = lane_row < SSM_HEADDIM
    dt = dt_ref[...]
    a_row = jnp.where(lane_row < SSM_HEADS, -jnp.exp(alog_ref[...]), 0.0)
    acs = _cumsum_rows(dt * a_row)
    acs_t = acs.T
    row = lax.broadcasted_iota(jnp.int32, (q, q), 0)
    col = lax.broadcasted_iota(jnp.int32, (q, q), 1)
    causal = row >= col

    inner = SSM_HEADS * SSM_HEADDIM
    assert SSM_HEADDIM & (SSM_HEADDIM - 1) == 0
    head_of_lane = lax.broadcasted_iota(jnp.int32, (LANES, inner), 1) >> (SSM_HEADDIM.bit_length() - 1)
    replicate = jnp.where(head_of_lane == lax.broadcasted_iota(jnp.int32, (LANES, inner), 0), 1.0, 0.0).astype(BF16)
    acs_x = None
    for part in _split3(acs):
        term = _mm(part, replicate)
        acs_x = term if acs_x is None else acs_x + term
    tot_x = acs_x[q - 1:q, :]
    grow_x = jnp.exp(acs_x)
    dend_x = jnp.exp(tot_x - acs_x)
    dec_x = jnp.exp(tot_x)

    groups = []
    for g in range(SSM_GROUPS):
        bmat = bm_ref[:, g * SSM_STATE:(g + 1) * SSM_STATE]
        cmat = cm_ref[:, g * SSM_STATE:(g + 1) * SSM_STATE]
        cb = lax.dot_general(cmat, bmat, (((1,), (1,)), ((), ())), preferred_element_type=F32)
        prev_t = state[g]
        y_off = _mm(cmat, prev_t.astype(BF16))
        groups.append((bmat, cb, prev_t, y_off))

    for g, (bmat, cb, prev_t, y_off) in enumerate(groups):
        gcols = slice(g * gw, (g + 1) * gw)
        xd_parts, u_parts = [], []
        for p in range(pairs_per_group):
            c0 = g * gw + p * LANES
            h0 = c0 // SSM_HEADDIM
            pcols = slice(c0, c0 + LANES)
            x2 = xs_ref[:, pcols]
            dt2 = jnp.where(first_head, dt[:, h0:h0 + 1], dt[:, h0 + 1:h0 + 2])
            xdt2 = x2 * dt2
            xdt2_b = xdt2.astype(BF16)
            ydiag = []
            for e in (0, 1):
                seg = jnp.exp(jnp.where(causal, acs[:, h0 + e:h0 + e + 1] - acs_t[h0 + e:h0 + e + 1, :], -jnp.inf))
                ydiag.append(_mm((cb * seg).astype(BF16), xdt2_b))
            y2 = jnp.where(first_head, ydiag[0], ydiag[1]) + y_off[:, p * LANES:(p + 1) * LANES] * grow_x[:, pcols]
            xd_parts.append((xdt2 * dend_x[:, pcols]).astype(BF16))
            y2 = y2 + dskip_ref[:, pcols] * x2
            zz = z_ref[:, pcols]
            u_parts.append(y2 * (zz * _sigmoid(zz)))
        xd = jnp.concatenate(xd_parts, axis=1)
        new_t = lax.dot_general(bmat, xd, (((0,), (0,)), ((), ())), preferred_element_type=F32)
        state[g] = prev_t * dec_x[:, gcols] + new_t
        u = jnp.concatenate(u_parts, axis=1)
        ms = jnp.mean(u * u, axis=-1, keepdims=True)
        o_ref[:, gcols] = (u * lax.rsqrt(ms + LN_EPS) * ng_ref[:, gcols]).astype(o_ref.dtype)


def _ssd_call(xs, bm, cm, dt, z, alog_row, dskip_row, ng_row, *, batch, seq):
    t, inner = xs.shape
    nc = seq // SSM_CHUNK
    q = SSM_CHUNK
    gw = inner // SSM_GROUPS
    rows = lambda b, c: (b * nc + c, 0)
    const = lambda b, c: (0, 0)
    return pl.pallas_call(
        _ssd_kernel,
        grid=(batch, nc),
        in_specs=[pl.BlockSpec((q, inner), rows),
                  pl.BlockSpec((q, bm.shape[1]), rows),
                  pl.BlockSpec((q, cm.shape[1]), rows),
                  pl.BlockSpec((q, LANES), rows),
                  pl.BlockSpec((q, inner), rows),
                  pl.BlockSpec((1, LANES), const),
                  pl.BlockSpec((1, inner), const),
                  pl.BlockSpec((1, inner), const)],
        out_specs=pl.BlockSpec((q, inner), rows),
        out_shape=jax.ShapeDtypeStruct((t, inner), BF16),
        scratch_shapes=[pltpu.VMEM((SSM_GROUPS, SSM_STATE, gw), F32)],
        compiler_params=_params(("arbitrary", "arbitrary")),
        name="ssd_scan",
    )(xs, bm, cm, dt, z, alog_row, dskip_row, ng_row)


def _split3(x):
    hi = x.astype(BF16)
    r1 = x - hi.astype(F32)
    mid = r1.astype(BF16)
    lo = (r1 - mid.astype(F32)).astype(BF16)
    return hi, mid, lo


def _l1_in_kernel(x_ref, w_ref, wf_ref, bf_ref, qkv_ref, eq_ref, ek_ref, carry, *, tiles_per_seq, q_width):
    seq_start = pl.program_id(0) % tiles_per_seq == 0
    items = []
    for cols in _chunks(qkv_ref.shape[1]):
        def store(y, cols=cols):
            if cols.stop <= q_width:
                y = y * (FOX_HEADDIM ** -0.5 * LOG2E)
            qkv_ref[:, cols] = y.astype(qkv_ref.dtype)
        items.append((lambda cols=cols: _mm(x_ref[...], w_ref[:, cols]), store))
    _issue_ahead(items)
    f = _mm(x_ref[...], wf_ref[...]) + bf_ref[...]
    cum = _cumsum_rows(-_softplus(-f)) + jnp.where(seq_start, 0.0, carry[...])
    carry[...] = cum[cum.shape[0] - 1:, :]
    parts = _split3(cum * LOG2E)
    r = lax.broadcasted_iota(jnp.int32, (LANES, LANES), 0)
    l = lax.broadcasted_iota(jnp.int32, (LANES, LANES), 1)
    lane = lax.broadcasted_iota(jnp.int32, (1, LANES), 1) & (FOX_EXT - 1)
    spread_q = spread_k = None
    for c, part in enumerate(parts):
        sq = _mm(part, jnp.where(l == FOX_EXT * r + c, 1.0, 0.0).astype(BF16))
        sk = _mm(part, jnp.where(l == FOX_EXT * r + 3 + c, 1.0, 0.0).astype(BF16))
        spread_q = sq if spread_q is None else spread_q + sq
        spread_k = sk if spread_k is None else spread_k + sk
    ones_q = jnp.where((lane >= 3) & (lane < 6), 1.0, 0.0)
    ones_k = jnp.where(lane < 3, 1.0, 0.0)
    eq_ref[...] = (spread_q + ones_q).astype(BF16)
    ek_ref[...] = (ones_k - spread_k).astype(BF16)


def _l1_in_call(xb, w_qkv, w_f, b_f, *, seq):
    t, d = xb.shape
    n = w_qkv.shape[1]
    return pl.pallas_call(
        functools.partial(_l1_in_kernel, tiles_per_seq=seq // ROW_TILE, q_width=n // 3),
        grid=(t // ROW_TILE,),
        in_specs=[_row_spec(d), _resident(w_qkv.shape), _resident(w_f.shape), _resident(b_f.shape)],
        out_specs=[_row_spec(n), _row_spec(LANES), _row_spec(LANES)],
        out_shape=[jax.ShapeDtypeStruct((t, n), BF16), jax.ShapeDtypeStruct((t, LANES), BF16),
                   jax.ShapeDtypeStruct((t, LANES), BF16)],
        scratch_shapes=[pltpu.VMEM((1, LANES), F32)],
        compiler_params=_params(("arbitrary",)),
        name="l1_in",
    )(xb, w_qkv, w_f, b_f)


def _fox_attn_kernel(q_ref, k_ref, v_ref, eq_ref, ek_ref, o_ref, vt_ref, *, tq, nq):
    hp = pl.program_id(1)
    lane2 = lax.broadcasted_iota(jnp.int32, (1, 2 * LANES), 1)
    head_lanes = []
    for e in (0, 1):
        ext0 = LANES + FOX_EXT * (2 * hp + e)
        head_lanes.append(((lane2 >= FOX_HEADDIM * e) & (lane2 < FOX_HEADDIM * (e + 1)))
                          | ((lane2 >= ext0) & (lane2 < ext0 + FOX_EXT)))
    key = lax.broadcasted_iota(jnp.int32, (tq, tq), 0)
    qry = lax.broadcasted_iota(jnp.int32, (tq, tq), 1)
    visible = key <= qry

    vrows = FOX_HEADDIM + FOX_ONES
    for j in range(nq):
        v_t = v_ref[j * tq:(j + 1) * tq, :].astype(F32).T.astype(BF16)
        for e in (0, 1):
            vt_ref[j, e, 0:FOX_HEADDIM, :] = v_t[FOX_HEADDIM * e:FOX_HEADDIM * (e + 1), :]
            vt_ref[j, e, FOX_HEADDIM:vrows, :] = jnp.ones((FOX_ONES, tq), BF16)

    def head_queries(i):
        rows = slice(i * tq, (i + 1) * tq)
        qe = jnp.concatenate([q_ref[rows, :], eq_ref[rows, :]], axis=1)
        return [jnp.where(m, qe, jnp.zeros_like(qe)) for m in head_lanes]

    def logits(qm, j):
        rows = slice(j * tq, (j + 1) * tq)
        ke = jnp.concatenate([k_ref[rows, :], ek_ref[rows, :]], axis=1)
        return tuple(lax.dot_general(ke, qm[e], (((1,), (1,)), ((), ())), preferred_element_type=F32)
                     for e in (0, 1))

    def softmax(s, m, diagonal):
        if diagonal:
            s = [jnp.where(visible, se, NEG_BIG) for se in s]
        m_new = tuple(jnp.maximum(me, jnp.max(se, axis=0, keepdims=True)) for se, me in zip(s, m))
        alpha = tuple(jnp.exp2(me - mn) for me, mn in zip(m, m_new))
        p = tuple(jnp.exp2(se - mn).astype(BF16) for se, mn in zip(s, m_new))
        return m_new, alpha, p

    def weighted_values(j, p):
        return [_mm(vt_ref[j, e], p[e]) for e in (0, 1)]

    steps = [(i, j) for i in range(nq) for j in range(i + 1)]
    queries = {}

    def masked_queries(i):
        if i not in queries:
            queries[i] = head_queries(i)
        return queries[i]

    def finish(i, acc):
        o_t = jnp.concatenate([ac[0:FOX_HEADDIM] / ac[FOX_HEADDIM:FOX_HEADDIM + 1] for ac in acc], axis=0)
        o_ref[i * tq:(i + 1) * tq, :] = o_t.T.astype(o_ref.dtype)

    def retire(pending, accs):
        (pi, pj, p, alpha), pv = pending
        accs[pi] = pv if pj == 0 else [a * ac + pve for a, ac, pve in zip(alpha, accs[pi], pv)]
        if pj == pi:
            finish(pi, accs.pop(pi))

    accs = {}
    pending = None
    s_next = logits(masked_queries(0), 0)
    m = None
    for k, (i, j) in enumerate(steps):
        s = s_next
        if k + 1 < len(steps):
            s_next = logits(masked_queries(steps[k + 1][0]), steps[k + 1][1])
        if pending is not None:
            pending = (pending, weighted_values(pending[1], pending[2]))
        if j == 0:
            m = tuple(jnp.full((1, tq), NEG_BIG, F32) for _ in (0, 1))
        m, alpha, p = softmax(s, m, j == i)
        if pending is not None:
            retire(pending, accs)
        pending = (i, j, p, alpha)
    retire((pending, weighted_values(pending[1], pending[2])), accs)


def _fox_attn_call(qkv, eq, ek, *, batch, seq, tq):
    t = qkv.shape[0]
    width = FOX_HEADS * FOX_HEADDIM
    npairs = width // LANES
    blk = lambda off: pl.BlockSpec((seq, LANES), lambda b, hp, o=off: (b, o + hp))
    ext = pl.BlockSpec((seq, LANES), lambda b, hp: (b, 0))
    return pl.pallas_call(
        functools.partial(_fox_attn_kernel, tq=tq, nq=seq // tq),
        grid=(batch, npairs),
        in_specs=[blk(0), blk(npairs), blk(2 * npairs), ext, ext],
        out_specs=pl.BlockSpec((seq, LANES), lambda b, hp: (b, hp)),
        out_shape=jax.ShapeDtypeStruct((t, width), BF16),
        scratch_shapes=[pltpu.VMEM((seq // tq, 2, FOX_HEADDIM + FOX_ONES, tq), BF16)],
        compiler_params=_params(("arbitrary", "arbitrary")),
        name="fox_attn",
    )(qkv, qkv, qkv, eq, ek)


def _pad_cols(a, width):
    return jnp.pad(a, ((0, 0), (0, width - a.shape[1])))


def kernel(x, sc_ssm_w_in, sc_conv_w, ssm_conv_w, ssm_conv_b, ssm_dt_bias, ssm_a_log, ssm_d, ssm_norm_g,
           sc_ssm_w_out, fox_w_in, fox_b_f, fox_w_out, ffn_w_up, ffn_conv_w, ffn_conv_b, ffn_w_down,
           ln_mix_g, ln_mix_b, ln_ffn_g, ln_ffn_b):
    batch, seq, d = x.shape
    t = batch * seq
    x32 = x.reshape(t, d)
    w_up, w_down = ffn_w_up.astype(BF16), ffn_w_down.astype(BF16)

    inner = SSM_HEADS * SSM_HEADDIM
    bc_w = SSM_GROUPS * SSM_STATE
    o_dt = 3 * d + 2 * inner + 2 * bc_w
    ya, z, xs, bm, cm, dt = _l0_in_call(
        x32, sc_ssm_w_in[0].astype(BF16), _pad_cols(sc_ssm_w_in[0][:, o_dt:], LANES).astype(BF16),
        sc_conv_w[0], ssm_conv_w[0], ssm_conv_b[0].reshape(1, -1), _pad_cols(ssm_dt_bias[0].reshape(1, -1), LANES),
        seq=seq, inner=inner, bc_w=bc_w)
    yb = _ssd_call(xs, bm, cm, dt, z,
                   _pad_cols(ssm_a_log[0].reshape(1, -1), LANES),
                   jnp.repeat(ssm_d[0], SSM_HEADDIM).reshape(1, -1),
                   ssm_norm_g[0].reshape(1, -1), batch=batch, seq=seq)
    x32, xb = _outproj_ln([ya, yb], sc_ssm_w_out[0].astype(BF16), x32, ln_mix_g[0], ln_mix_b[0], name="l0_out")
    x32, xb = _conv_ffn(x32, xb, w_up, ffn_conv_w[0], ffn_conv_b[0], w_down, ln_ffn_g[0], ln_ffn_b[0],
                        layer=0, seq=seq, name="l0_ffn")

    width = FOX_HEADS * FOX_HEADDIM
    fw = fox_w_in[0]
    qkv, eq, ek = _l1_in_call(xb, fw[:, :3 * width].astype(BF16), _pad_cols(fw[:, 3 * width:], LANES).astype(BF16),
                              _pad_cols(fox_b_f[0].reshape(1, -1), LANES), seq=seq)
    o = _fox_attn_call(qkv, eq, ek, batch=batch, seq=seq, tq=256)
    x32, xb = _outproj_ln([o], fox_w_out[0].astype(BF16), x32, ln_mix_g[1], ln_mix_b[1], name="l1_out")
    x32, xb = _conv_ffn(x32, xb, w_up, ffn_conv_w[1], ffn_conv_b[1], w_down, ln_ffn_g[1], ln_ffn_b[1],
                        layer=1, seq=seq, name="l1_ffn")
    return x32.reshape(batch, seq, d)
```

```python
import functools

import jax
import jax.numpy as jnp
from jax import lax
from jax.experimental import pallas as pl
from jax.experimental.pallas import tpu as pltpu

F32 = jnp.float32
BF16 = jnp.bfloat16

DEPTH = 2
ALPHA = (2 * DEPTH) ** 0.25
LN_EPS = 1e-5

SSM_HEADDIM = 64
SSM_HEADS = 16
SSM_GROUPS = 2
SSM_STATE = 128
SSM_CHUNK = 128

FOX_HEADS = 16
FOX_HEADDIM = 64
FOX_EXT = 8
FOX_ONES = 16
LOG2E = 1.4426950408889634

LANES = 128
ROW_TILE = 512
COL_CHUNK = 256
CONV_HALO = 8
NEG_BIG = -1e30
VMEM_LIMIT = 56 * 1024 * 1024


def _params(sem):
    return pltpu.CompilerParams(dimension_semantics=sem, vmem_limit_bytes=VMEM_LIMIT)


def _mm(a, b):
    return jnp.dot(a, b, preferred_element_type=F32)


def _sigmoid(x):
    return 1.0 / (1.0 + jnp.exp(-x))


def _softplus(x):
    return jnp.maximum(x, 0.0) + jnp.log1p(jnp.exp(-jnp.abs(x)))


def _cumsum_rows(x):
    n = x.shape[0]
    row = lax.broadcasted_iota(jnp.int32, x.shape, 0)
    s = 1
    while s < n:
        x = x + jnp.where(row >= s, pltpu.roll(x, s, 0), 0.0)
        s *= 2
    return x


def _chunks(width):
    assert width % COL_CHUNK == 0
    return [slice(c, c + COL_CHUNK) for c in range(0, width, COL_CHUNK)]


def _shift(cols, off):
    return slice(cols.start + off, cols.stop + off)


def _causal_conv(val, halo, cbuf, slot, cols, w, taps, seq_start):
    tm = val.shape[0]
    cbuf[slot, 0:CONV_HALO, :] = jnp.where(seq_start, 0.0, halo[:, cols])
    cbuf[slot, CONV_HALO:CONV_HALO + tm, :] = val
    halo[:, cols] = val[tm - CONV_HALO:tm, :]
    out = val * w[taps - 1:taps, :]
    for k in range(taps - 1):
        sh = taps - 1 - k
        out = out + cbuf[slot, CONV_HALO - sh:CONV_HALO - sh + tm, :] * w[k:k + 1, :]
    return out


def _issue_ahead(items):
    ahead = items[0][0]()
    for k, (_, finish) in enumerate(items):
        raw = ahead
        if k + 1 < len(items):
            ahead = items[k + 1][0]()
        finish(raw)


def _layer_norm(y, g, b):
    mu = jnp.mean(y, axis=-1, keepdims=True)
    d = y - mu
    var = jnp.mean(d * d, axis=-1, keepdims=True)
    return d * lax.rsqrt(var + LN_EPS) * g + b


def _resident(shape):
    return pl.BlockSpec(shape, lambda i: (0,) * len(shape), pipeline_mode=pl.Buffered(1))


def _resident_layer(shape, layer):
    return pl.BlockSpec((None,) + tuple(shape[1:]), lambda i: (layer,) + (0,) * (len(shape) - 1),
                        pipeline_mode=pl.Buffered(1))


def _row_spec(cols):
    return pl.BlockSpec((ROW_TILE, cols), lambda i: (i, 0))


def _l0_in_kernel(x_ref, w_ref, wdt_ref, scw_ref, xcw_ref, xcb_ref, dtb_ref,
                  ya_ref, z_ref, xs_ref, bm_ref, cm_ref, dt_ref,
                  halo_sc, halo_x, cbuf, xb_ref, *, tiles_per_seq, d, inner, bc_w):
    seq_start = pl.program_id(0) % tiles_per_seq == 0
    xb_ref[...] = x_ref[...].astype(BF16)

    def proj(off, cols):
        return _mm(xb_ref[...], w_ref[:, _shift(cols, off)])

    items = []
    for n, cols in enumerate(_chunks(d)):
        def sconv_done(raw, n=n, cols=cols):
            gate, c, h = raw
            y = _causal_conv(c * h, halo_sc, cbuf, n % 2, cols, scw_ref[:, cols], scw_ref.shape[0], seq_start)
            ya_ref[:, cols] = (gate * y).astype(ya_ref.dtype)
        items.append((lambda cols=cols: (proj(0, cols), proj(d, cols), proj(2 * d, cols)), sconv_done))
    for cols in _chunks(inner):
        def z_done(raw, cols=cols):
            z_ref[:, cols] = raw.astype(z_ref.dtype)
        items.append((lambda cols=cols: proj(3 * d, cols), z_done))
    for n, cols in enumerate(_chunks(inner + 2 * bc_w)):
        def xbc_done(raw, n=n, cols=cols):
            y = _causal_conv(raw, halo_x, cbuf, n % 2, cols, xcw_ref[:, cols], xcw_ref.shape[0], seq_start) + xcb_ref[:, cols]
            y = y * _sigmoid(y)
            if cols.stop <= inner:
                xs_ref[:, cols] = y.astype(xs_ref.dtype)
            elif cols.stop <= inner + bc_w:
                bm_ref[:, _shift(cols, -inner)] = y.astype(bm_ref.dtype)
            else:
                cm_ref[:, _shift(cols, -inner - bc_w)] = y.astype(cm_ref.dtype)
        items.append((lambda cols=cols: proj(3 * d + inner, cols), xbc_done))

    def dt_done(raw):
        dt_ref[...] = _softplus(raw + dtb_ref[...])
    items.append((lambda: _mm(xb_ref[...], wdt_ref[...]), dt_done))
    _issue_ahead(items)


def _l0_in_call(x32, w_in, w_dt, sc_cw, x_cw, x_cb, dt_b, *, seq, inner, bc_w):
    t, d = x32.shape
    assert bc_w == COL_CHUNK
    kern = functools.partial(_l0_in_kernel, tiles_per_seq=seq // ROW_TILE, d=d, inner=inner, bc_w=bc_w)
    return pl.pallas_call(
        kern,
        grid=(t // ROW_TILE,),
        in_specs=[_row_spec(d), _resident(w_in.shape), _resident(w_dt.shape), _resident(sc_cw.shape),
                  _resident(x_cw.shape), _resident(x_cb.shape), _resident(dt_b.shape)],
        out_specs=[_row_spec(d), _row_spec(inner), _row_spec(inner), _row_spec(bc_w), _row_spec(bc_w), _row_spec(LANES)],
        out_shape=[jax.ShapeDtypeStruct((t, d), BF16), jax.ShapeDtypeStruct((t, inner), F32),
                   jax.ShapeDtypeStruct((t, inner), F32), jax.ShapeDtypeStruct((t, bc_w), BF16),
                   jax.ShapeDtypeStruct((t, bc_w), BF16), jax.ShapeDtypeStruct((t, LANES), F32)],
        scratch_shapes=[pltpu.VMEM((CONV_HALO, d), F32), pltpu.VMEM((CONV_HALO, inner + 2 * bc_w), F32),
                        pltpu.VMEM((2, ROW_TILE + CONV_HALO, COL_CHUNK), F32),
                        pltpu.VMEM((ROW_TILE, d), BF16)],
        compiler_params=_params(("arbitrary",)),
        name="l0_in",
    )(x32, w_in, w_dt, sc_cw, x_cw, x_cb, dt_b)


def _outproj_ln_kernel(*refs, n_in):
    a_refs = refs[:n_in]
    w_refs = refs[n_in:2 * n_in]
    xres_ref, g_ref, b_ref, o32_ref = refs[2 * n_in:]
    tm = xres_ref.shape[0]
    n_sub = 2

    def project(rows):
        acc = _mm(a_refs[0][rows, :], w_refs[0][...])
        for a_ref, w_ref in zip(a_refs[1:], w_refs[1:]):
            acc = acc + _mm(a_ref[rows, :], w_ref[...])
        return acc

    def normalize(acc, rows):
        out = _layer_norm(ALPHA * xres_ref[rows, :] + acc, g_ref[...], b_ref[...])
        o32_ref[rows, :] = out

    items = []
    for r in range(n_sub):
        rows = slice(r * tm // n_sub, (r + 1) * tm // n_sub)
        items.append((functools.partial(project, rows), functools.partial(normalize, rows=rows)))
    _issue_ahead(items)


def _outproj_ln(acts, w, xres, g, b, *, name):
    t, d = xres.shape
    width = acts[0].shape[1]
    assert all(a.shape[1] == width for a in acts) and w.shape[0] == width * len(acts)
    in_specs = [_row_spec(width) for _ in acts]
    in_specs += [pl.BlockSpec((width, d), lambda i, k=k: (k, 0), pipeline_mode=pl.Buffered(1)) for k in range(len(acts))]
    in_specs += [_row_spec(d), _resident((1, d)), _resident((1, d))]
    weights = [w] * len(acts)
    return pl.pallas_call(
        functools.partial(_outproj_ln_kernel, n_in=len(acts)),
        grid=(t // ROW_TILE,),
        in_specs=in_specs,
        out_specs=_row_spec(d),
        out_shape=jax.ShapeDtypeStruct((t, d), F32),
        compiler_params=_params(("arbitrary",)),
        name=name,
    )(*acts, *weights, xres, g.reshape(1, d), b.reshape(1, d))


def _ffn_kernel(xres_ref, wup_ref, cw_ref, cb_ref, wd_ref, g_ref, b_ref, *rest, tiles_per_seq, d_ff):
    (o32_ref, *maybe_o16), (halo, cbuf, act_ref, x_ref) = rest[:-4], rest[-4:]
    seq_start = pl.program_id(0) % tiles_per_seq == 0
    taps = cw_ref.shape[0]
    chunks = _chunks(d_ff)
    x_ref[...] = xres_ref[...].astype(BF16)

    items = []
    for n, cols in enumerate(chunks):
        def gate(raw, n=n, cols=cols):
            gcols = _shift(cols, d_ff)
            u = _causal_conv(raw[0], halo, cbuf, 2 * (n % 2), cols, cw_ref[:, cols], taps, seq_start) + cb_ref[:, cols]
            gt = _causal_conv(raw[1], halo, cbuf, 2 * (n % 2) + 1, gcols, cw_ref[:, gcols], taps, seq_start) + cb_ref[:, gcols]
            act_ref[:, cols] = (u * (gt * _sigmoid(gt))).astype(BF16)
        items.append((lambda cols=cols: (_mm(x_ref[...], wup_ref[:, cols]),
                                         _mm(x_ref[...], wup_ref[:, _shift(cols, d_ff)])), gate))
    _issue_ahead(items)
    out = _layer_norm(ALPHA * xres_ref[...] + _mm(act_ref[...], wd_ref[...]), g_ref[...], b_ref[...])
    o32_ref[...] = out
    for o16_ref in maybe_o16:
        o16_ref[...] = out.astype(BF16)


def _conv_ffn(x32, w_up, conv_w, conv_b, w_down, g, b, *, layer, seq, name, emit_bf16):
    t, d = x32.shape
    d_ff = w_down.shape[1]
    cb = conv_b.reshape(1, -1)
    n_out = 2 if emit_bf16 else 1
    return pl.pallas_call(
        functools.partial(_ffn_kernel, tiles_per_seq=seq // ROW_TILE, d_ff=d_ff),
        grid=(t // ROW_TILE,),
        in_specs=[_row_spec(d), _resident_layer(w_up.shape, layer), _resident(conv_w.shape),
                  _resident(cb.shape), _resident_layer(w_down.shape, layer), _resident((1, d)), _resident((1, d))],
        out_specs=[_row_spec(d), _row_spec(d)][:n_out],
        out_shape=[jax.ShapeDtypeStruct((t, d), F32), jax.ShapeDtypeStruct((t, d), BF16)][:n_out],
        scratch_shapes=[pltpu.VMEM((CONV_HALO, 2 * d_ff), F32),
                        pltpu.VMEM((4, ROW_TILE + CONV_HALO, COL_CHUNK), F32),
                        pltpu.VMEM((ROW_TILE, d_ff), BF16),
                        pltpu.VMEM((ROW_TILE, d), BF16)],
        compiler_params=_params(("arbitrary",)),
        name=name,
    )(x32, w_up, conv_w, cb, w_down, g.reshape(1, d), b.reshape(1, d))


def _ssd_kernel(xs_ref, bm_ref, cm_ref, dt_ref, z_ref, alog_ref, dskip_ref, ng_ref, o_ref, state):
    q = SSM_CHUNK
    gw = (SSM_HEADS // SSM_GROUPS) * SSM_HEADDIM
    pairs_per_group = gw // LANES

    @pl.when(pl.program_id(1) == 0)
    def _():
        state[...] = jnp.zeros(state.shape, F32)

    lane_row = lax.broadcasted_iota(jnp.int32, (1, LANES), 1)
    first_head = lane_row < SSM_HEADDIM
    dt = dt_ref[...]
    a_row = jnp.where(lane_row < SSM_HEADS, -jnp.exp(alog_ref[...]), 0.0)
    acs = _cumsum_rows(dt * a_row)
    acs_t = acs.T
    row = lax.broadcasted_iota(jnp.int32, (q, q), 0)
    col = lax.broadcasted_iota(jnp.int32, (q, q), 1)
    causal = row >= col

    inner = SSM_HEADS * SSM_HEADDIM
    assert SSM_HEADDIM & (SSM_HEADDIM - 1) == 0
    head_of_lane = lax.broadcasted_iota(jnp.int32, (LANES, inner), 1) >> (SSM_HEADDIM.bit_length() - 1)
    replicate = jnp.where(head_of_lane == lax.broadcasted_iota(jnp.int32, (LANES, inner), 0), 1.0, 0.0).astype(BF16)
    acs_x = None
    for part in _split3(acs):
        term = _mm(part, replicate)
        acs_x = term if acs_x is None else acs_x + term
    tot_x = acs_x[q - 1:q, :]
    grow_x = jnp.exp(acs_x)
    dend_x = jnp.exp(tot_x - acs_x)
    dec_x = jnp.exp(tot_x)

    groups = []
    for g in range(SSM_GROUPS):
        bmat = bm_ref[:, g * SSM_STATE:(g + 1) * SSM_STATE]
        cmat = cm_ref[:, g * SSM_STATE:(g + 1) * SSM_STATE]
        cb = lax.dot_general(cmat, bmat, (((1,), (1,)), ((), ())), preferred_element_type=F32)
        prev_t = state[g]
        y_off = _mm(cmat, prev_t.astype(BF16))
        groups.append((bmat, cb, prev_t, y_off))

    for g, (bmat, cb, prev_t, y_off) in enumerate(groups):
        gcols = slice(g * gw, (g + 1) * gw)
        xd_parts, u_parts = [], []
        for p in range(pairs_per_group):
            c0 = g * gw + p * LANES
            h0 = c0 // SSM_HEADDIM
            pcols = slice(c0, c0 + LANES)
            x2 = xs_ref[:, pcols]
            dt2 = jnp.where(first_head, dt[:, h0:h0 + 1], dt[:, h0 + 1:h0 + 2])
            xdt2 = x2 * dt2
            xdt2_b = xdt2.astype(BF16)
            ydiag = []
            for e in (0, 1):
                seg = jnp.exp(jnp.where(causal, acs[:, h0 + e:h0 + e + 1] - acs_t[h0 + e:h0 + e + 1, :], -jnp.inf))
                ydiag.append(_mm((cb * seg).astype(BF16), xdt2_b))
            y2 = jnp.where(first_head, ydiag[0], ydiag[1]) + y_off[:, p * LANES:(p + 1) * LANES] * grow_x[:, pcols]
            xd_parts.append((xdt2 * dend_x[:, pcols]).astype(BF16))
            y2 = y2 + dskip_ref[:, pcols] * x2
            zz = z_ref[:, pcols]
            u_parts.append(y2 * (zz * _sigmoid(zz)))
        xd = jnp.concatenate(xd_parts, axis=1)
        new_t = lax.dot_general(bmat, xd, (((0,), (0,)), ((), ())), preferred_element_type=F32)
        state[g] = prev_t * dec_x[:, gcols] + new_t
        u = jnp.concatenate(u_parts, axis=1)
        ms = jnp.mean(u * u, axis=-1, keepdims=True)
        o_ref[:, gcols] = (u * lax.rsqrt(ms + LN_EPS) * ng_ref[:, gcols]).astype(o_ref.dtype)


def _ssd_call(xs, bm, cm, dt, z, alog_row, dskip_row, ng_row, *, batch, seq):
    t, inner = xs.shape
    nc = seq // SSM_CHUNK
    q = SSM_CHUNK
    gw = inner // SSM_GROUPS
    rows = lambda b, c: (b * nc + c, 0)
    const = lambda b, c: (0, 0)
    return pl.pallas_call(
        _ssd_kernel,
        grid=(batch, nc),
        in_specs=[pl.BlockSpec((q, inner), rows),
                  pl.BlockSpec((q, bm.shape[1]), rows),
                  pl.BlockSpec((q, cm.shape[1]), rows),
                  pl.BlockSpec((q, LANES), rows),
                  pl.BlockSpec((q, inner), rows),
                  pl.BlockSpec((1, LANES), const),
                  pl.BlockSpec((1, inner), const),
                  pl.BlockSpec((1, inner), const)],
        out_specs=pl.BlockSpec((q, inner), rows),
        out_shape=jax.ShapeDtypeStruct((t, inner), BF16),
        scratch_shapes=[pltpu.VMEM((SSM_GROUPS, SSM_STATE, gw), F32)],
        compiler_params=_params(("arbitrary", "arbitrary")),
        name="ssd_scan",
    )(xs, bm, cm, dt, z, alog_row, dskip_row, ng_row)


def _split3(x):
    hi = x.astype(BF16)
    r1 = x - hi.astype(F32)
    mid = r1.astype(BF16)
    lo = (r1 - mid.astype(F32)).astype(BF16)
    return hi, mid, lo


def _l1_in_kernel(x_ref, w_ref, wf_ref, bf_ref, qkv_ref, eq_ref, ek_ref, carry, *, tiles_per_seq, q_width):
    seq_start = pl.program_id(0) % tiles_per_seq == 0
    items = []
    for cols in _chunks(qkv_ref.shape[1]):
        def store(y, cols=cols):
            if cols.stop <= q_width:
                y = y * (FOX_HEADDIM ** -0.5 * LOG2E)
            qkv_ref[:, cols] = y.astype(qkv_ref.dtype)
        items.append((lambda cols=cols: _mm(x_ref[...], w_ref[:, cols]), store))
    _issue_ahead(items)
    f = _mm(x_ref[...], wf_ref[...]) + bf_ref[...]
    cum = _cumsum_rows(-_softplus(-f)) + jnp.where(seq_start, 0.0, carry[...])
    carry[...] = cum[cum.shape[0] - 1:, :]
    parts = _split3(cum * LOG2E)
    r = lax.broadcasted_iota(jnp.int32, (LANES, LANES), 0)
    l = lax.broadcasted_iota(jnp.int32, (LANES, LANES), 1)
    lane = lax.broadcasted_iota(jnp.int32, (1, LANES), 1) & (FOX_EXT - 1)
    spread_q = spread_k = None
    for c, part in enumerate(parts):
        sq = _mm(part, jnp.where(l == FOX_EXT * r + c, 1.0, 0.0).astype(BF16))
        sk = _mm(part, jnp.where(l == FOX_EXT * r + 3 + c, 1.0, 0.0).astype(BF16))
        spread_q = sq if spread_q is None else spread_q + sq
        spread_k = sk if spread_k is None else spread_k + sk
    ones_q = jnp.where((lane >= 3) & (lane < 6), 1.0, 0.0)
    ones_k = jnp.where(lane < 3, 1.0, 0.0)
    eq_ref[...] = (spread_q + ones_q).astype(BF16)
    ek_ref[...] = (ones_k - spread_k).astype(BF16)


def _l1_in_call(xb, w_qkv, w_f, b_f, *, seq):
    t, d = xb.shape
    n = w_qkv.shape[1]
    return pl.pallas_call(
        functools.partial(_l1_in_kernel, tiles_per_seq=seq // ROW_TILE, q_width=n // 3),
        grid=(t // ROW_TILE,),
        in_specs=[_row_spec(d), _resident(w_qkv.shape), _resident(w_f.shape), _resident(b_f.shape)],
        out_specs=[_row_spec(n), _row_spec(LANES), _row_spec(LANES)],
        out_shape=[jax.ShapeDtypeStruct((t, n), BF16), jax.ShapeDtypeStruct((t, LANES), BF16),
                   jax.ShapeDtypeStruct((t, LANES), BF16)],
        scratch_shapes=[pltpu.VMEM((1, LANES), F32)],
        compiler_params=_params(("arbitrary",)),
        name="l1_in",
    )(xb, w_qkv, w_f, b_f)


def _fox_attn_kernel(q_ref, k_ref, v_ref, eq_ref, ek_ref, o_ref, vt_ref, *, tq, nq):
    hp = pl.program_id(1)
    lane2 = lax.broadcasted_iota(jnp.int32, (1, 2 * LANES), 1)
    head_lanes = []
    for e in (0, 1):
        ext0 = LANES + FOX_EXT * (2 * hp + e)
        head_lanes.append(((lane2 >= FOX_HEADDIM * e) & (lane2 < FOX_HEADDIM * (e + 1)))
                          | ((lane2 >= ext0) & (lane2 < ext0 + FOX_EXT)))
    key = lax.broadcasted_iota(jnp.int32, (tq, tq), 0)
    qry = lax.broadcasted_iota(jnp.int32, (tq, tq), 1)
    visible = key <= qry

    vrows = FOX_HEADDIM + FOX_ONES
    for j in range(nq):
        v_t = v_ref[j * tq:(j + 1) * tq, :].astype(F32).T.astype(BF16)
        for e in (0, 1):
            vt_ref[j, e, 0:FOX_HEADDIM, :] = v_t[FOX_HEADDIM * e:FOX_HEADDIM * (e + 1), :]
            vt_ref[j, e, FOX_HEADDIM:vrows, :] = jnp.ones((FOX_ONES, tq), BF16)

    def head_queries(i):
        rows = slice(i * tq, (i + 1) * tq)
        qe = jnp.concatenate([q_ref[rows, :], eq_ref[rows, :]], axis=1)
        return [jnp.where(m, qe, jnp.zeros_like(qe)) for m in head_lanes]

    def logits(qm, j):
        rows = slice(j * tq, (j + 1) * tq)
        ke = jnp.concatenate([k_ref[rows, :], ek_ref[rows, :]], axis=1)
        return lax.dot_general(ke, qm, (((1,), (1,)), ((), ())), preferred_element_type=F32)

    def softmax(s, m, diagonal):
        if diagonal:
            s = jnp.where(visible, s, NEG_BIG)
        m_new = jnp.maximum(m, jnp.max(s, axis=0, keepdims=True))
        return m_new, jnp.exp2(m - m_new), jnp.exp2(s - m_new).astype(BF16)

    steps = [(i, j) for i in range(nq) for j in range(i + 1)]
    queries = {}

    def masked_queries(i):
        if i not in queries:
            queries[i] = head_queries(i)
        return queries[i]

    outs = {}

    def finish(i, e, acc):
        outs.setdefault(i, {})[e] = acc[0:FOX_HEADDIM] / acc[FOX_HEADDIM:FOX_HEADDIM + 1]
        if len(outs[i]) == 2:
            o_t = jnp.concatenate([outs[i][0], outs[i][1]], axis=0)
            o_ref[i * tq:(i + 1) * tq, :] = o_t.T.astype(o_ref.dtype)
            del outs[i]

    class Stream:
        def __init__(self, e, steps):
            self.e = e
            self.steps = steps
            self.s_next = logits(masked_queries(steps[0][0])[e], steps[0][1])
            self.pending = None
            self.acc = {}
            self.m = None

        def retire(self, pending, pv):
            pi, pj, _, alpha = pending
            self.acc[pi] = pv if pj == 0 else alpha * self.acc[pi] + pv
            if pj == pi:
                finish(pi, self.e, self.acc.pop(pi))

        def step(self, k):
            if k >= len(self.steps):
                return
            i, j = self.steps[k]
            s = self.s_next
            if k + 1 < len(self.steps):
                self.s_next = logits(masked_queries(self.steps[k + 1][0])[self.e], self.steps[k + 1][1])
            pv = None if self.pending is None else _mm(vt_ref[self.pending[1], self.e], self.pending[2])
            if j == 0:
                self.m = jnp.full((1, tq), NEG_BIG, F32)
            self.m, alpha, p = softmax(s, self.m, j == i)
            if self.pending is not None:
                self.retire(self.pending, pv)
            self.pending = (i, j, p, alpha)

        def drain(self):
            self.retire(self.pending, _mm(vt_ref[self.pending[1], self.e], self.pending[2]))

    group = lambda i: (i % 4) in (0, 3)
    assert nq % 4 == 0
    lo = [st for st in steps if group(st[0])]
    hi = [st for st in steps if not group(st[0])]
    streams = [Stream(0, lo), Stream(1, lo), Stream(0, hi), Stream(1, hi)]
    for k in range(max(len(stream.steps) for stream in streams)):
        for stream in streams:
            stream.step(k)
    for stream in streams:
        stream.drain()


def _fox_attn_call(qkv, eq, ek, *, batch, seq, tq):
    t = qkv.shape[0]
    width = FOX_HEADS * FOX_HEADDIM
    npairs = width // LANES
    blk = lambda off: pl.BlockSpec((seq, LANES), lambda b, hp, o=off: (b, o + hp))
    ext = pl.BlockSpec((seq, LANES), lambda b, hp: (b, 0))
    return pl.pallas_call(
        functools.partial(_fox_attn_kernel, tq=tq, nq=seq // tq),
        grid=(batch, npairs),
        in_specs=[blk(0), blk(npairs), blk(2 * npairs), ext, ext],
        out_specs=pl.BlockSpec((seq, LANES), lambda b, hp: (b, hp)),
        out_shape=jax.ShapeDtypeStruct((t, width), BF16),
        scratch_shapes=[pltpu.VMEM((seq // tq, 2, FOX_HEADDIM + FOX_ONES, tq), BF16)],
        compiler_params=_params(("arbitrary", "arbitrary")),
        name="fox_attn",
    )(qkv, qkv, qkv, eq, ek)


def _pad_cols(a, width):
    return jnp.pad(a, ((0, 0), (0, width - a.shape[1])))


def kernel(x, sc_ssm_w_in, sc_conv_w, ssm_conv_w, ssm_conv_b, ssm_dt_bias, ssm_a_log, ssm_d, ssm_norm_g,
           sc_ssm_w_out, fox_w_in, fox_b_f, fox_w_out, ffn_w_up, ffn_conv_w, ffn_conv_b, ffn_w_down,
           ln_mix_g, ln_mix_b, ln_ffn_g, ln_ffn_b):
    batch, seq, d = x.shape
    t = batch * seq
    x32 = x.reshape(t, d)
    w_up, w_down = ffn_w_up.astype(BF16), ffn_w_down.astype(BF16)

    inner = SSM_HEADS * SSM_HEADDIM
    bc_w = SSM_GROUPS * SSM_STATE
    o_dt = 3 * d + 2 * inner + 2 * bc_w
    ya, z, xs, bm, cm, dt = _l0_in_call(
        x32, sc_ssm_w_in[0].astype(BF16), _pad_cols(sc_ssm_w_in[0][:, o_dt:], LANES).astype(BF16),
        sc_conv_w[0], ssm_conv_w[0], ssm_conv_b[0].reshape(1, -1), _pad_cols(ssm_dt_bias[0].reshape(1, -1), LANES),
        seq=seq, inner=inner, bc_w=bc_w)
    yb = _ssd_call(xs, bm, cm, dt, z,
                   _pad_cols(ssm_a_log[0].reshape(1, -1), LANES),
                   jnp.repeat(ssm_d[0], SSM_HEADDIM).reshape(1, -1),
                   ssm_norm_g[0].reshape(1, -1), batch=batch, seq=seq)
    x32 = _outproj_ln([ya, yb], sc_ssm_w_out[0].astype(BF16), x32, ln_mix_g[0], ln_mix_b[0], name="l0_out")
    x32, xb = _conv_ffn(x32, w_up, ffn_conv_w[0], ffn_conv_b[0], w_down, ln_ffn_g[0], ln_ffn_b[0],
                        layer=0, seq=seq, name="l0_ffn", emit_bf16=True)

    width = FOX_HEADS * FOX_HEADDIM
    fw = fox_w_in[0]
    qkv, eq, ek = _l1_in_call(xb, fw[:, :3 * width].astype(BF16), _pad_cols(fw[:, 3 * width:], LANES).astype(BF16),
                              _pad_cols(fox_b_f[0].reshape(1, -1), LANES), seq=seq)
    o = _fox_attn_call(qkv, eq, ek, batch=batch, seq=seq, tq=256)
    x32 = _outproj_ln([o], fox_w_out[0].astype(BF16), x32, ln_mix_g[1], ln_mix_b[1], name="l1_out")
    (x32,) = _conv_ffn(x32, w_up, ffn_conv_w[1], ffn_conv_b[1], w_down, ln_ffn_g[1], ln_ffn_b[1],
                       layer=1, seq=seq, name="l1_ffn", emit_bf16=False)
    return x32.reshape(batch, seq, d)
```

```python
import functools

import jax
import jax.numpy as jnp
from jax import lax
from jax.experimental import pallas as pl
from jax.experimental.pallas import tpu as pltpu

F32 = jnp.float32
BF16 = jnp.bfloat16

DEPTH = 2
ALPHA = (2 * DEPTH) ** 0.25
LN_EPS = 1e-5

SSM_HEADDIM = 64
SSM_HEADS = 16
SSM_GROUPS = 2
SSM_STATE = 128
SSM_CHUNK = 128
SSD_CHUNKS_PER_STEP = 4

FOX_HEADS = 16
FOX_HEADDIM = 64
FOX_EXT = 8
FOX_ONES = 16
LOG2E = 1.4426950408889634

LANES = 128
ROW_TILE = 512
COL_CHUNK = 256
CONV_HALO = 8
NEG_BIG = -1e30
VMEM_LIMIT = 56 * 1024 * 1024


def _params(sem):
    return pltpu.CompilerParams(dimension_semantics=sem, vmem_limit_bytes=VMEM_LIMIT)


def _mm(a, b):
    return jnp.dot(a, b, preferred_element_type=F32)


def _sigmoid(x):
    return 1.0 / (1.0 + jnp.exp(-x))


def _softplus(x):
    return jnp.maximum(x, 0.0) + jnp.log1p(jnp.exp(-jnp.abs(x)))


def _cumsum_rows(x):
    n = x.shape[0]
    row = lax.broadcasted_iota(jnp.int32, x.shape, 0)
    s = 1
    while s < n:
        x = x + jnp.where(row >= s, pltpu.roll(x, s, 0), 0.0)
        s *= 2
    return x


def _chunks(width):
    assert width % COL_CHUNK == 0
    return [slice(c, c + COL_CHUNK) for c in range(0, width, COL_CHUNK)]


def _shift(cols, off):
    return slice(cols.start + off, cols.stop + off)


def _causal_conv(val, halo, cbuf, slot, cols, w, taps, seq_start):
    tm = val.shape[0]
    cbuf[slot, 0:CONV_HALO, :] = jnp.where(seq_start, 0.0, halo[:, cols])
    cbuf[slot, CONV_HALO:CONV_HALO + tm, :] = val
    halo[:, cols] = val[tm - CONV_HALO:tm, :]
    out = val * w[taps - 1:taps, :]
    for k in range(taps - 1):
        sh = taps - 1 - k
        out = out + cbuf[slot, CONV_HALO - sh:CONV_HALO - sh + tm, :] * w[k:k + 1, :]
    return out


def _issue_ahead(items):
    ahead = items[0][0]()
    for k, (_, finish) in enumerate(items):
        raw = ahead
        if k + 1 < len(items):
            ahead = items[k + 1][0]()
        finish(raw)


def _layer_norm(y, g, b):
    mu = jnp.mean(y, axis=-1, keepdims=True)
    d = y - mu
    var = jnp.mean(d * d, axis=-1, keepdims=True)
    return d * lax.rsqrt(var + LN_EPS) * g + b


def _resident(shape):
    return pl.BlockSpec(shape, lambda i: (0,) * len(shape), pipeline_mode=pl.Buffered(1))


def _resident_layer(shape, layer):
    return pl.BlockSpec((None,) + tuple(shape[1:]), lambda i: (layer,) + (0,) * (len(shape) - 1),
                        pipeline_mode=pl.Buffered(1))


def _row_spec(cols):
    return pl.BlockSpec((ROW_TILE, cols), lambda i: (i, 0))


def _l0_in_kernel(x_ref, w_ref, wdt_ref, scw_ref, xcw_ref, xcb_ref, dtb_ref,
                  ya_ref, z_ref, xs_ref, bm_ref, cm_ref, dt_ref,
                  halo_sc, halo_x, cbuf, xb_ref, *, tiles_per_seq, d, inner, bc_w):
    seq_start = pl.program_id(0) % tiles_per_seq == 0
    xb_ref[...] = x_ref[...].astype(BF16)

    def proj(off, cols):
        return _mm(xb_ref[...], w_ref[:, _shift(cols, off)])

    items = []
    for n, cols in enumerate(_chunks(d)):
        def sconv_done(raw, n=n, cols=cols):
            gate, c, h = raw
            y = _causal_conv(c * h, halo_sc, cbuf, n % 2, cols, scw_ref[:, cols], scw_ref.shape[0], seq_start)
            ya_ref[:, cols] = (gate * y).astype(ya_ref.dtype)
        items.append((lambda cols=cols: (proj(0, cols), proj(d, cols), proj(2 * d, cols)), sconv_done))
    for cols in _chunks(inner):
        def z_done(raw, cols=cols):
            z_ref[:, cols] = raw.astype(z_ref.dtype)
        items.append((lambda cols=cols: proj(3 * d, cols), z_done))
    for n, cols in enumerate(_chunks(inner + 2 * bc_w)):
        def xbc_done(raw, n=n, cols=cols):
            y = _causal_conv(raw, halo_x, cbuf, n % 2, cols, xcw_ref[:, cols], xcw_ref.shape[0], seq_start) + xcb_ref[:, cols]
            y = y * _sigmoid(y)
            if cols.stop <= inner:
                xs_ref[:, cols] = y.astype(xs_ref.dtype)
            elif cols.stop <= inner + bc_w:
                bm_ref[:, _shift(cols, -inner)] = y.astype(bm_ref.dtype)
            else:
                cm_ref[:, _shift(cols, -inner - bc_w)] = y.astype(cm_ref.dtype)
        items.append((lambda cols=cols: proj(3 * d + inner, cols), xbc_done))

    def dt_done(raw):
        dt_ref[...] = _softplus(raw + dtb_ref[...])
    items.append((lambda: _mm(xb_ref[...], wdt_ref[...]), dt_done))
    _issue_ahead(items)


def _l0_in_call(x32, w_in, w_dt, sc_cw, x_cw, x_cb, dt_b, *, seq, inner, bc_w):
    t, d = x32.shape
    assert bc_w == COL_CHUNK
    kern = functools.partial(_l0_in_kernel, tiles_per_seq=seq // ROW_TILE, d=d, inner=inner, bc_w=bc_w)
    return pl.pallas_call(
        kern,
        grid=(t // ROW_TILE,),
        in_specs=[_row_spec(d), _resident(w_in.shape), _resident(w_dt.shape), _resident(sc_cw.shape),
                  _resident(x_cw.shape), _resident(x_cb.shape), _resident(dt_b.shape)],
        out_specs=[_row_spec(d), _row_spec(inner), _row_spec(inner), _row_spec(bc_w), _row_spec(bc_w), _row_spec(LANES)],
        out_shape=[jax.ShapeDtypeStruct((t, d), BF16), jax.ShapeDtypeStruct((t, inner), F32),
                   jax.ShapeDtypeStruct((t, inner), F32), jax.ShapeDtypeStruct((t, bc_w), BF16),
                   jax.ShapeDtypeStruct((t, bc_w), BF16), jax.ShapeDtypeStruct((t, LANES), F32)],
        scratch_shapes=[pltpu.VMEM((CONV_HALO, d), F32), pltpu.VMEM((CONV_HALO, inner + 2 * bc_w), F32),
                        pltpu.VMEM((2, ROW_TILE + CONV_HALO, COL_CHUNK), F32),
                        pltpu.VMEM((ROW_TILE, d), BF16)],
        compiler_params=_params(("arbitrary",)),
        name="l0_in",
    )(x32, w_in, w_dt, sc_cw, x_cw, x_cb, dt_b)


def _outproj_ln_kernel(*refs, n_in):
    a_refs = refs[:n_in]
    w_refs = refs[n_in:2 * n_in]
    xres_ref, g_ref, b_ref, o32_ref = refs[2 * n_in:]
    tm = xres_ref.shape[0]
    n_sub = 2

    def project(rows):
        acc = _mm(a_refs[0][rows, :], w_refs[0][...])
        for a_ref, w_ref in zip(a_refs[1:], w_refs[1:]):
            acc = acc + _mm(a_ref[rows, :], w_ref[...])
        return acc

    def normalize(acc, rows):
        out = _layer_norm(ALPHA * xres_ref[rows, :] + acc, g_ref[...], b_ref[...])
        o32_ref[rows, :] = out

    items = []
    for r in range(n_sub):
        rows = slice(r * tm // n_sub, (r + 1) * tm // n_sub)
        items.append((functools.partial(project, rows), functools.partial(normalize, rows=rows)))
    _issue_ahead(items)


def _outproj_ln(acts, w, xres, g, b, *, name):
    t, d = xres.shape
    width = acts[0].shape[1]
    assert all(a.shape[1] == width for a in acts) and w.shape[0] == width * len(acts)
    in_specs = [_row_spec(width) for _ in acts]
    in_specs += [pl.BlockSpec((width, d), lambda i, k=k: (k, 0), pipeline_mode=pl.Buffered(1)) for k in range(len(acts))]
    in_specs += [_row_spec(d), _resident((1, d)), _resident((1, d))]
    weights = [w] * len(acts)
    return pl.pallas_call(
        functools.partial(_outproj_ln_kernel, n_in=len(acts)),
        grid=(t // ROW_TILE,),
        in_specs=in_specs,
        out_specs=_row_spec(d),
        out_shape=jax.ShapeDtypeStruct((t, d), F32),
        compiler_params=_params(("arbitrary",)),
        name=name,
    )(*acts, *weights, xres, g.reshape(1, d), b.reshape(1, d))


def _ffn_kernel(xres_ref, wup_ref, cw_ref, cb_ref, wd_ref, g_ref, b_ref, *rest, tiles_per_seq, d_ff):
    (o32_ref, *maybe_o16), (halo, cbuf, act_ref, x_ref) = rest[:-4], rest[-4:]
    seq_start = pl.program_id(0) % tiles_per_seq == 0
    taps = cw_ref.shape[0]
    chunks = _chunks(d_ff)
    x_ref[...] = xres_ref[...].astype(BF16)

    items = []
    for n, cols in enumerate(chunks):
        def gate(raw, n=n, cols=cols):
            gcols = _shift(cols, d_ff)
            u = _causal_conv(raw[0], halo, cbuf, 2 * (n % 2), cols, cw_ref[:, cols], taps, seq_start) + cb_ref[:, cols]
            gt = _causal_conv(raw[1], halo, cbuf, 2 * (n % 2) + 1, gcols, cw_ref[:, gcols], taps, seq_start) + cb_ref[:, gcols]
            act_ref[:, cols] = (u * (gt * _sigmoid(gt))).astype(BF16)
        items.append((lambda cols=cols: (_mm(x_ref[...], wup_ref[:, cols]),
                                         _mm(x_ref[...], wup_ref[:, _shift(cols, d_ff)])), gate))
    _issue_ahead(items)
    out = _layer_norm(ALPHA * xres_ref[...] + _mm(act_ref[...], wd_ref[...]), g_ref[...], b_ref[...])
    o32_ref[...] = out
    for o16_ref in maybe_o16:
        o16_ref[...] = out.astype(BF16)


def _conv_ffn(x32, w_up, conv_w, conv_b, w_down, g, b, *, layer, seq, name, emit_bf16):
    t, d = x32.shape
    d_ff = w_down.shape[1]
    cb = conv_b.reshape(1, -1)
    n_out = 2 if emit_bf16 else 1
    return pl.pallas_call(
        functools.partial(_ffn_kernel, tiles_per_seq=seq // ROW_TILE, d_ff=d_ff),
        grid=(t // ROW_TILE,),
        in_specs=[_row_spec(d), _resident_layer(w_up.shape, layer), _resident(conv_w.shape),
                  _resident(cb.shape), _resident_layer(w_down.shape, layer), _resident((1, d)), _resident((1, d))],
        out_specs=[_row_spec(d), _row_spec(d)][:n_out],
        out_shape=[jax.ShapeDtypeStruct((t, d), F32), jax.ShapeDtypeStruct((t, d), BF16)][:n_out],
        scratch_shapes=[pltpu.VMEM((CONV_HALO, 2 * d_ff), F32),
                        pltpu.VMEM((4, ROW_TILE + CONV_HALO, COL_CHUNK), F32),
                        pltpu.VMEM((ROW_TILE, d_ff), BF16),
                        pltpu.VMEM((ROW_TILE, d), BF16)],
        compiler_params=_params(("arbitrary",)),
        name=name,
    )(x32, w_up, conv_w, cb, w_down, g.reshape(1, d), b.reshape(1, d))


def _ssd_kernel(xs_ref, bm_ref, cm_ref, dt_ref, z_ref, alog_ref, dskip_ref, ng_ref, o_ref, state):
    q = SSM_CHUNK

    @pl.when(pl.program_id(1) == 0)
    def _():
        state[...] = jnp.zeros(state.shape, F32)

    for sub in range(SSD_CHUNKS_PER_STEP):
        _ssd_chunk(slice(sub * q, (sub + 1) * q), xs_ref, bm_ref, cm_ref, dt_ref, z_ref, alog_ref, dskip_ref,
                   ng_ref, o_ref, state)


def _ssd_chunk(rows, xs_ref, bm_ref, cm_ref, dt_ref, z_ref, alog_ref, dskip_ref, ng_ref, o_ref, state):
    q = SSM_CHUNK
    gw = (SSM_HEADS // SSM_GROUPS) * SSM_HEADDIM
    pairs_per_group = gw // LANES
    lane_row = lax.broadcasted_iota(jnp.int32, (1, LANES), 1)
    first_head = lane_row < SSM_HEADDIM
    dt = dt_ref[rows, :]
    a_row = jnp.where(lane_row < SSM_HEADS, -jnp.exp(alog_ref[...]), 0.0)
    acs = _cumsum_rows(dt * a_row)
    acs_t = acs.T
    row = lax.broadcasted_iota(jnp.int32, (q, q), 0)
    col = lax.broadcasted_iota(jnp.int32, (q, q), 1)
    causal = row >= col

    inner = SSM_HEADS * SSM_HEADDIM
    assert SSM_HEADDIM & (SSM_HEADDIM - 1) == 0
    head_of_lane = lax.broadcasted_iota(jnp.int32, (LANES, inner), 1) >> (SSM_HEADDIM.bit_length() - 1)
    replicate = jnp.where(head_of_lane == lax.broadcasted_iota(jnp.int32, (LANES, inner), 0), 1.0, 0.0).astype(BF16)
    acs_x = None
    for part in _split3(acs):
        term = _mm(part, replicate)
        acs_x = term if acs_x is None else acs_x + term
    tot_x = acs_x[q - 1:q, :]
    grow_x = jnp.exp(acs_x)
    dend_x = jnp.exp(tot_x - acs_x)
    dec_x = jnp.exp(tot_x)

    groups = []
    for g in range(SSM_GROUPS):
        bmat = bm_ref[rows, g * SSM_STATE:(g + 1) * SSM_STATE]
        cmat = cm_ref[rows, g * SSM_STATE:(g + 1) * SSM_STATE]
        cb = lax.dot_general(cmat, bmat, (((1,), (1,)), ((), ())), preferred_element_type=F32)
        prev_t = state[g]
        y_off = _mm(cmat, prev_t.astype(BF16))
        groups.append((bmat, cb, prev_t, y_off))

    for g, (bmat, cb, prev_t, y_off) in enumerate(groups):
        gcols = slice(g * gw, (g + 1) * gw)
        xd_parts, u_parts = [], []
        for p in range(pairs_per_group):
            c0 = g * gw + p * LANES
            h0 = c0 // SSM_HEADDIM
            pcols = slice(c0, c0 + LANES)
            x2 = xs_ref[rows, pcols]
            dt2 = jnp.where(first_head, dt[:, h0:h0 + 1], dt[:, h0 + 1:h0 + 2])
            xdt2 = x2 * dt2
            xdt2_b = xdt2.astype(BF16)
            ydiag = []
            for e in (0, 1):
                seg = jnp.exp(jnp.where(causal, acs[:, h0 + e:h0 + e + 1] - acs_t[h0 + e:h0 + e + 1, :], -jnp.inf))
                ydiag.append(_mm((cb * seg).astype(BF16), xdt2_b))
            y2 = jnp.where(first_head, ydiag[0], ydiag[1]) + y_off[:, p * LANES:(p + 1) * LANES] * grow_x[:, pcols]
            xd_parts.append((xdt2 * dend_x[:, pcols]).astype(BF16))
            y2 = y2 + dskip_ref[:, pcols] * x2
            zz = z_ref[rows, pcols]
            u_parts.append(y2 * (zz * _sigmoid(zz)))
        xd = jnp.concatenate(xd_parts, axis=1)
        new_t = lax.dot_general(bmat, xd, (((0,), (0,)), ((), ())), preferred_element_type=F32)
        state[g] = prev_t * dec_x[:, gcols] + new_t
        u = jnp.concatenate(u_parts, axis=1)
        ms = jnp.mean(u * u, axis=-1, keepdims=True)
        o_ref[rows, gcols] = (u * lax.rsqrt(ms + LN_EPS) * ng_ref[:, gcols]).astype(o_ref.dtype)


def _ssd_call(xs, bm, cm, dt, z, alog_row, dskip_row, ng_row, *, batch, seq):
    t, inner = xs.shape
    q = SSM_CHUNK * SSD_CHUNKS_PER_STEP
    nc = seq // q
    gw = inner // SSM_GROUPS
    rows = lambda b, c: (b * nc + c, 0)
    const = lambda b, c: (0, 0)
    return pl.pallas_call(
        _ssd_kernel,
        grid=(batch, nc),
        in_specs=[pl.BlockSpec((q, inner), rows),
                  pl.BlockSpec((q, bm.shape[1]), rows),
                  pl.BlockSpec((q, cm.shape[1]), rows),
                  pl.BlockSpec((q, LANES), rows),
                  pl.BlockSpec((q, inner), rows),
                  pl.BlockSpec((1, LANES), const),
                  pl.BlockSpec((1, inner), const),
                  pl.BlockSpec((1, inner), const)],
        out_specs=pl.BlockSpec((q, inner), rows),
        out_shape=jax.ShapeDtypeStruct((t, inner), BF16),
        scratch_shapes=[pltpu.VMEM((SSM_GROUPS, SSM_STATE, gw), F32)],
        compiler_params=_params(("arbitrary", "arbitrary")),
        name="ssd_scan",
    )(xs, bm, cm, dt, z, alog_row, dskip_row, ng_row)


def _split3(x):
    hi = x.astype(BF16)
    r1 = x - hi.astype(F32)
    mid = r1.astype(BF16)
    lo = (r1 - mid.astype(F32)).astype(BF16)
    return hi, mid, lo


def _l1_in_kernel(x_ref, w_ref, wf_ref, bf_ref, qkv_ref, eq_ref, ek_ref, carry, *, tiles_per_seq, q_width):
    seq_start = pl.program_id(0) % tiles_per_seq == 0
    items = []
    for cols in _chunks(qkv_ref.shape[1]):
        def store(y, cols=cols):
            if cols.stop <= q_width:
                y = y * (FOX_HEADDIM ** -0.5 * LOG2E)
            qkv_ref[:, cols] = y.astype(qkv_ref.dtype)
        items.append((lambda cols=cols: _mm(x_ref[...], w_ref[:, cols]), store))
    _issue_ahead(items)
    f = _mm(x_ref[...], wf_ref[...]) + bf_ref[...]
    cum = _cumsum_rows(-_softplus(-f)) + jnp.where(seq_start, 0.0, carry[...])
    carry[...] = cum[cum.shape[0] - 1:, :]
    parts = _split3(cum * LOG2E)
    r = lax.broadcasted_iota(jnp.int32, (LANES, LANES), 0)
    l = lax.broadcasted_iota(jnp.int32, (LANES, LANES), 1)
    lane = lax.broadcasted_iota(jnp.int32, (1, LANES), 1) & (FOX_EXT - 1)
    spread_q = spread_k = None
    for c, part in enumerate(parts):
        sq = _mm(part, jnp.where(l == FOX_EXT * r + c, 1.0, 0.0).astype(BF16))
        sk = _mm(part, jnp.where(l == FOX_EXT * r + 3 + c, 1.0, 0.0).astype(BF16))
        spread_q = sq if spread_q is None else spread_q + sq
        spread_k = sk if spread_k is None else spread_k + sk
    ones_q = jnp.where((lane >= 3) & (lane < 6), 1.0, 0.0)
    ones_k = jnp.where(lane < 3, 1.0, 0.0)
    eq_ref[...] = (spread_q + ones_q).astype(BF16)
    ek_ref[...] = (ones_k - spread_k).astype(BF16)


def _l1_in_call(xb, w_qkv, w_f, b_f, *, seq):
    t, d = xb.shape
    n = w_qkv.shape[1]
    return pl.pallas_call(
        functools.partial(_l1_in_kernel, tiles_per_seq=seq // ROW_TILE, q_width=n // 3),
        grid=(t // ROW_TILE,),
        in_specs=[_row_spec(d), _resident(w_qkv.shape), _resident(w_f.shape), _resident(b_f.shape)],
        out_specs=[_row_spec(n), _row_spec(LANES), _row_spec(LANES)],
        out_shape=[jax.ShapeDtypeStruct((t, n), BF16), jax.ShapeDtypeStruct((t, LANES), BF16),
                   jax.ShapeDtypeStruct((t, LANES), BF16)],
        scratch_shapes=[pltpu.VMEM((1, LANES), F32)],
        compiler_params=_params(("arbitrary",)),
        name="l1_in",
    )(xb, w_qkv, w_f, b_f)


def _fox_attn_kernel(q_ref, k_ref, v_ref, eq_ref, ek_ref, o_ref, vt_ref, *, tq, nq):
    hp = pl.program_id(1)
    lane2 = lax.broadcasted_iota(jnp.int32, (1, 2 * LANES), 1)
    head_lanes = []
    for e in (0, 1):
        ext0 = LANES + FOX_EXT * (2 * hp + e)
        head_lanes.append(((lane2 >= FOX_HEADDIM * e) & (lane2 < FOX_HEADDIM * (e + 1)))
                          | ((lane2 >= ext0) & (lane2 < ext0 + FOX_EXT)))
    key = lax.broadcasted_iota(jnp.int32, (tq, tq), 0)
    qry = lax.broadcasted_iota(jnp.int32, (tq, tq), 1)
    visible = key <= qry

    vrows = FOX_HEADDIM + FOX_ONES
    for j in range(nq):
        v_t = v_ref[j * tq:(j + 1) * tq, :].astype(F32).T.astype(BF16)
        for e in (0, 1):
            vt_ref[j, e, 0:FOX_HEADDIM, :] = v_t[FOX_HEADDIM * e:FOX_HEADDIM * (e + 1), :]
            vt_ref[j, e, FOX_HEADDIM:vrows, :] = jnp.ones((FOX_ONES, tq), BF16)

    def head_queries(i):
        rows = slice(i * tq, (i + 1) * tq)
        qe = jnp.concatenate([q_ref[rows, :], eq_ref[rows, :]], axis=1)
        return [jnp.where(m, qe, jnp.zeros_like(qe)) for m in head_lanes]

    def logits(qm, j):
        rows = slice(j * tq, (j + 1) * tq)
        ke = jnp.concatenate([k_ref[rows, :], ek_ref[rows, :]], axis=1)
        return lax.dot_general(ke, qm, (((1,), (1,)), ((), ())), preferred_element_type=F32)

    def softmax(s, m, diagonal):
        if diagonal:
            s = jnp.where(visible, s, NEG_BIG)
        m_new = jnp.maximum(m, jnp.max(s, axis=0, keepdims=True))
        return m_new, jnp.exp2(m - m_new), jnp.exp2(s - m_new).astype(BF16)

    steps = [(i, j) for i in range(nq) for j in range(i + 1)]
    queries = {}

    def masked_queries(i):
        if i not in queries:
            queries[i] = head_queries(i)
        return queries[i]

    outs = {}

    def finish(i, e, acc):
        outs.setdefault(i, {})[e] = acc[0:FOX_HEADDIM] / acc[FOX_HEADDIM:FOX_HEADDIM + 1]
        if len(outs[i]) == 2:
            o_t = jnp.concatenate([outs[i][0], outs[i][1]], axis=0)
            o_ref[i * tq:(i + 1) * tq, :] = o_t.T.astype(o_ref.dtype)
            del outs[i]

    class Stream:
        def __init__(self, e, steps):
            self.e = e
            self.steps = steps
            self.s_next = logits(masked_queries(steps[0][0])[e], steps[0][1])
            self.pending = None
            self.acc = {}
            self.m = None

        def retire(self, pending, pv):
            pi, pj, _, alpha = pending
            self.acc[pi] = pv if pj == 0 else alpha * self.acc[pi] + pv
            if pj == pi:
                finish(pi, self.e, self.acc.pop(pi))

        def step(self, k):
            if k >= len(self.steps):
                return
            i, j = self.steps[k]
            s = self.s_next
            if k + 1 < len(self.steps):
                self.s_next = logits(masked_queries(self.steps[k + 1][0])[self.e], self.steps[k + 1][1])
            pv = None if self.pending is None else _mm(vt_ref[self.pending[1], self.e], self.pending[2])
            if j == 0:
                self.m = jnp.full((1, tq), NEG_BIG, F32)
            self.m, alpha, p = softmax(s, self.m, j == i)
            if self.pending is not None:
                self.retire(self.pending, pv)
            self.pending = (i, j, p, alpha)

        def drain(self):
            self.retire(self.pending, _mm(vt_ref[self.pending[1], self.e], self.pending[2]))

    group = lambda i: (i % 4) in (0, 3)
    assert nq % 4 == 0
    lo = [st for st in steps if group(st[0])]
    hi = [st for st in steps if not group(st[0])]
    streams = [Stream(0, lo), Stream(1, lo), Stream(0, hi), Stream(1, hi)]
    for k in range(max(len(stream.steps) for stream in streams)):
        for stream in streams:
            stream.step(k)
    for stream in streams:
        stream.drain()


def _fox_attn_call(qkv, eq, ek, *, batch, seq, tq):
    t = qkv.shape[0]
    width = FOX_HEADS * FOX_HEADDIM
    npairs = width // LANES
    blk = lambda off: pl.BlockSpec((seq, LANES), lambda b, hp, o=off: (b, o + hp))
    ext = pl.BlockSpec((seq, LANES), lambda b, hp: (b, 0))
    return pl.pallas_call(
        functools.partial(_fox_attn_kernel, tq=tq, nq=seq // tq),
        grid=(batch, npairs),
        in_specs=[blk(0), blk(npairs), blk(2 * npairs), ext, ext],
        out_specs=pl.BlockSpec((seq, LANES), lambda b, hp: (b, hp)),
        out_shape=jax.ShapeDtypeStruct((t, width), BF16),
        scratch_shapes=[pltpu.VMEM((seq // tq, 2, FOX_HEADDIM + FOX_ONES, tq), BF16)],
        compiler_params=_params(("arbitrary", "arbitrary")),
        name="fox_attn",
    )(qkv, qkv, qkv, eq, ek)


def _pad_cols(a, width):
    return jnp.pad(a, ((0, 0), (0, width - a.shape[1])))


def kernel(x, sc_ssm_w_in, sc_conv_w, ssm_conv_w, ssm_conv_b, ssm_dt_bias, ssm_a_log, ssm_d, ssm_norm_g,
           sc_ssm_w_out, fox_w_in, fox_b_f, fox_w_out, ffn_w_up, ffn_conv_w, ffn_conv_b, ffn_w_down,
           ln_mix_g, ln_mix_b, ln_ffn_g, ln_ffn_b):
    batch, seq, d = x.shape
    t = batch * seq
    x32 = x.reshape(t, d)
    w_up, w_down = ffn_w_up.astype(BF16), ffn_w_down.astype(BF16)

    inner = SSM_HEADS * SSM_HEADDIM
    bc_w = SSM_GROUPS * SSM_STATE
    o_dt = 3 * d + 2 * inner + 2 * bc_w
    ya, z, xs, bm, cm, dt = _l0_in_call(
        x32, sc_ssm_w_in[0].astype(BF16), _pad_cols(sc_ssm_w_in[0][:, o_dt:], LANES).astype(BF16),
        sc_conv_w[0], ssm_conv_w[0], ssm_conv_b[0].reshape(1, -1), _pad_cols(ssm_dt_bias[0].reshape(1, -1), LANES),
        seq=seq, inner=inner, bc_w=bc_w)
    yb = _ssd_call(xs, bm, cm, dt, z,
                   _pad_cols(ssm_a_log[0].reshape(1, -1), LANES),
                   jnp.repeat(ssm_d[0], SSM_HEADDIM).reshape(1, -1),
                   ssm_norm_g[0].reshape(1, -1), batch=batch, seq=seq)
    x32 = _outproj_ln([ya, yb], sc_ssm_w_out[0].astype(BF16), x32, ln_mix_g[0], ln_mix_b[0], name="l0_out")
    x32, xb = _conv_ffn(x32, w_up, ffn_conv_w[0], ffn_conv_b[0], w_down, ln_ffn_g[0], ln_ffn_b[0],
                        layer=0, seq=seq, name="l0_ffn", emit_bf16=True)

    width = FOX_HEADS * FOX_HEADDIM
    fw = fox_w_in[0]
    qkv, eq, ek = _l1_in_call(xb, fw[:, :3 * width].astype(BF16), _pad_cols(fw[:, 3 * width:], LANES).astype(BF16),
                              _pad_cols(fox_b_f[0].reshape(1, -1), LANES), seq=seq)
    o = _fox_attn_call(qkv, eq, ek, batch=batch, seq=seq, tq=256)
    x32 = _outproj_ln([o], fox_w_out[0].astype(BF16), x32, ln_mix_g[1], ln_mix_b[1], name="l1_out")
    (x32,) = _conv_ffn(x32, w_up, ffn_conv_w[1], ffn_conv_b[1], w_down, ln_ffn_g[1], ln_ffn_b[1],
                       layer=1, seq=seq, name="l1_ffn", emit_bf16=False)
    return x32.reshape(batch, seq, d)
```

```python
import functools

import jax
import jax.numpy as jnp
from jax import lax
from jax.experimental import pallas as pl
from jax.experimental.pallas import tpu as pltpu

F32 = jnp.float32
BF16 = jnp.bfloat16

DEPTH = 2
ALPHA = (2 * DEPTH) ** 0.25
LN_EPS = 1e-5

SSM_HEADDIM = 64
SSM_HEADS = 16
SSM_GROUPS = 2
SSM_STATE = 128
SSM_CHUNK = 128
SSD_CHUNKS_PER_STEP = 4

FOX_HEADS = 16
FOX_HEADDIM = 64
FOX_EXT = 8
FOX_ONES = 16
LOG2E = 1.4426950408889634

LANES = 128
ROW_TILE = 512
COL_CHUNK = 256
CONV_HALO = 8
NEG_BIG = -1e30
VMEM_LIMIT = 56 * 1024 * 1024


def _params(sem):
    return pltpu.CompilerParams(dimension_semantics=sem, vmem_limit_bytes=VMEM_LIMIT)


def _mm(a, b):
    return jnp.dot(a, b, preferred_element_type=F32)


def _sigmoid(x):
    return 1.0 / (1.0 + jnp.exp(-x))


def _softplus(x):
    return jnp.maximum(x, 0.0) + jnp.log1p(jnp.exp(-jnp.abs(x)))


def _cumsum_rows(x):
    n = x.shape[0]
    row = lax.broadcasted_iota(jnp.int32, x.shape, 0)
    s = 1
    while s < n:
        x = x + jnp.where(row >= s, pltpu.roll(x, s, 0), 0.0)
        s *= 2
    return x


def _chunks(width):
    assert width % COL_CHUNK == 0
    return [slice(c, c + COL_CHUNK) for c in range(0, width, COL_CHUNK)]


def _shift(cols, off):
    return slice(cols.start + off, cols.stop + off)


def _causal_conv(val, halo, cbuf, slot, cols, w, taps, seq_start):
    tm = val.shape[0]
    cbuf[slot, 0:CONV_HALO, :] = jnp.where(seq_start, 0.0, halo[:, cols])
    cbuf[slot, CONV_HALO:CONV_HALO + tm, :] = val
    halo[:, cols] = val[tm - CONV_HALO:tm, :]
    out = val * w[taps - 1:taps, :]
    for k in range(taps - 1):
        sh = taps - 1 - k
        out = out + cbuf[slot, CONV_HALO - sh:CONV_HALO - sh + tm, :] * w[k:k + 1, :]
    return out


def _issue_ahead(items):
    ahead = items[0][0]()
    for k, (_, finish) in enumerate(items):
        raw = ahead
        if k + 1 < len(items):
            ahead = items[k + 1][0]()
        finish(raw)


def _layer_norm(y, g, b):
    mu = jnp.mean(y, axis=-1, keepdims=True)
    d = y - mu
    var = jnp.mean(d * d, axis=-1, keepdims=True)
    return d * lax.rsqrt(var + LN_EPS) * g + b


def _resident(shape):
    return pl.BlockSpec(shape, lambda i: (0,) * len(shape), pipeline_mode=pl.Buffered(1))


def _resident_layer(shape, layer):
    return pl.BlockSpec((None,) + tuple(shape[1:]), lambda i: (layer,) + (0,) * (len(shape) - 1),
                        pipeline_mode=pl.Buffered(1))


def _row_spec(cols):
    return pl.BlockSpec((ROW_TILE, cols), lambda i: (i, 0))


def _l0_in_kernel(x_ref, w_ref, wdt_ref, scw_ref, xcw_ref, xcb_ref, dtb_ref,
                  ya_ref, z_ref, xs_ref, bm_ref, cm_ref, dt_ref,
                  halo_sc, halo_x, cbuf, xb_ref, *, tiles_per_seq, d, inner, bc_w):
    seq_start = pl.program_id(0) % tiles_per_seq == 0
    xb_ref[...] = x_ref[...].astype(BF16)

    def proj(off, cols):
        return _mm(xb_ref[...], w_ref[:, _shift(cols, off)])

    items = []
    for n, cols in enumerate(_chunks(d)):
        def sconv_done(raw, n=n, cols=cols):
            gate, c, h = raw
            y = _causal_conv(c * h, halo_sc, cbuf, n % 2, cols, scw_ref[:, cols], scw_ref.shape[0], seq_start)
            ya_ref[:, cols] = (gate * y).astype(ya_ref.dtype)
        items.append((lambda cols=cols: (proj(0, cols), proj(d, cols), proj(2 * d, cols)), sconv_done))
    for cols in _chunks(inner):
        def z_done(raw, cols=cols):
            z_ref[:, cols] = raw.astype(z_ref.dtype)
        items.append((lambda cols=cols: proj(3 * d, cols), z_done))
    for n, cols in enumerate(_chunks(inner + 2 * bc_w)):
        def xbc_done(raw, n=n, cols=cols):
            y = _causal_conv(raw, halo_x, cbuf, n % 2, cols, xcw_ref[:, cols], xcw_ref.shape[0], seq_start) + xcb_ref[:, cols]
            y = y * _sigmoid(y)
            if cols.stop <= inner:
                xs_ref[:, cols] = y.astype(xs_ref.dtype)
            elif cols.stop <= inner + bc_w:
                bm_ref[:, _shift(cols, -inner)] = y.astype(bm_ref.dtype)
            else:
                cm_ref[:, _shift(cols, -inner - bc_w)] = y.astype(cm_ref.dtype)
        items.append((lambda cols=cols: proj(3 * d + inner, cols), xbc_done))

    def dt_done(raw):
        dt_ref[...] = _softplus(raw + dtb_ref[...])
    items.append((lambda: _mm(xb_ref[...], wdt_ref[...]), dt_done))
    n_sc, n_z = len(_chunks(d)), len(_chunks(inner))
    sconv_items, z_items = items[:n_sc], items[n_sc:n_sc + n_z]
    xbc_items, dt_item = items[n_sc + n_z:-1], items[-1]
    light = [it for pair in zip(sconv_items, z_items) for it in pair]
    assert n_sc == n_z and len(xbc_items) <= len(light)
    order = [it for pair in zip(xbc_items, light) for it in pair] + light[len(xbc_items):] + [dt_item]
    _issue_ahead(order)


def _l0_in_call(x32, w_in, w_dt, sc_cw, x_cw, x_cb, dt_b, *, seq, inner, bc_w):
    t, d = x32.shape
    assert bc_w == COL_CHUNK
    kern = functools.partial(_l0_in_kernel, tiles_per_seq=seq // ROW_TILE, d=d, inner=inner, bc_w=bc_w)
    return pl.pallas_call(
        kern,
        grid=(t // ROW_TILE,),
        in_specs=[_row_spec(d), _resident(w_in.shape), _resident(w_dt.shape), _resident(sc_cw.shape),
                  _resident(x_cw.shape), _resident(x_cb.shape), _resident(dt_b.shape)],
        out_specs=[_row_spec(d), _row_spec(inner), _row_spec(inner), _row_spec(bc_w), _row_spec(bc_w), _row_spec(LANES)],
        out_shape=[jax.ShapeDtypeStruct((t, d), BF16), jax.ShapeDtypeStruct((t, inner), BF16),
                   jax.ShapeDtypeStruct((t, inner), BF16), jax.ShapeDtypeStruct((t, bc_w), BF16),
                   jax.ShapeDtypeStruct((t, bc_w), BF16), jax.ShapeDtypeStruct((t, LANES), F32)],
        scratch_shapes=[pltpu.VMEM((CONV_HALO, d), F32), pltpu.VMEM((CONV_HALO, inner + 2 * bc_w), F32),
                        pltpu.VMEM((2, ROW_TILE + CONV_HALO, COL_CHUNK), F32),
                        pltpu.VMEM((ROW_TILE, d), BF16)],
        compiler_params=_params(("arbitrary",)),
        name="l0_in",
    )(x32, w_in, w_dt, sc_cw, x_cw, x_cb, dt_b)


def _outproj_ln_kernel(*refs, n_in):
    a_refs = refs[:n_in]
    w_refs = refs[n_in:2 * n_in]
    xres_ref, g_ref, b_ref, o32_ref = refs[2 * n_in:]
    tm = xres_ref.shape[0]
    n_sub = 2

    def project(rows):
        acc = _mm(a_refs[0][rows, :], w_refs[0][...])
        for a_ref, w_ref in zip(a_refs[1:], w_refs[1:]):
            acc = acc + _mm(a_ref[rows, :], w_ref[...])
        return acc

    def normalize(acc, rows):
        out = _layer_norm(ALPHA * xres_ref[rows, :] + acc, g_ref[...], b_ref[...])
        o32_ref[rows, :] = out

    items = []
    for r in range(n_sub):
        rows = slice(r * tm // n_sub, (r + 1) * tm // n_sub)
        items.append((functools.partial(project, rows), functools.partial(normalize, rows=rows)))
    _issue_ahead(items)


def _outproj_ln(acts, w, xres, g, b, *, name):
    t, d = xres.shape
    width = acts[0].shape[1]
    assert all(a.shape[1] == width for a in acts) and w.shape[0] == width * len(acts)
    in_specs = [_row_spec(width) for _ in acts]
    in_specs += [pl.BlockSpec((width, d), lambda i, k=k: (k, 0), pipeline_mode=pl.Buffered(1)) for k in range(len(acts))]
    in_specs += [_row_spec(d), _resident((1, d)), _resident((1, d))]
    weights = [w] * len(acts)
    return pl.pallas_call(
        functools.partial(_outproj_ln_kernel, n_in=len(acts)),
        grid=(t // ROW_TILE,),
        in_specs=in_specs,
        out_specs=_row_spec(d),
        out_shape=jax.ShapeDtypeStruct((t, d), F32),
        compiler_params=_params(("arbitrary",)),
        name=name,
    )(*acts, *weights, xres, g.reshape(1, d), b.reshape(1, d))


def _ffn_kernel(xres_ref, wup_ref, cw_ref, cb_ref, wd_ref, g_ref, b_ref, *rest, tiles_per_seq, d_ff):
    (o32_ref, *maybe_o16), (halo, cbuf, act_ref, x_ref) = rest[:-4], rest[-4:]
    seq_start = pl.program_id(0) % tiles_per_seq == 0
    taps = cw_ref.shape[0]
    chunks = _chunks(d_ff)
    x_ref[...] = xres_ref[...].astype(BF16)

    items = []
    for n, cols in enumerate(chunks):
        def gate(raw, n=n, cols=cols):
            gcols = _shift(cols, d_ff)
            u = _causal_conv(raw[0], halo, cbuf, 2 * (n % 2), cols, cw_ref[:, cols], taps, seq_start) + cb_ref[:, cols]
            gt = _causal_conv(raw[1], halo, cbuf, 2 * (n % 2) + 1, gcols, cw_ref[:, gcols], taps, seq_start) + cb_ref[:, gcols]
            act_ref[:, cols] = (u * (gt * _sigmoid(gt))).astype(BF16)
        items.append((lambda cols=cols: (_mm(x_ref[...], wup_ref[:, cols]),
                                         _mm(x_ref[...], wup_ref[:, _shift(cols, d_ff)])), gate))
    _issue_ahead(items)
    out = _layer_norm(ALPHA * xres_ref[...] + _mm(act_ref[...], wd_ref[...]), g_ref[...], b_ref[...])
    o32_ref[...] = out
    for o16_ref in maybe_o16:
        o16_ref[...] = out.astype(BF16)


def _conv_ffn(x32, w_up, conv_w, conv_b, w_down, g, b, *, layer, seq, name, emit_bf16):
    t, d = x32.shape
    d_ff = w_down.shape[1]
    cb = conv_b.reshape(1, -1)
    n_out = 2 if emit_bf16 else 1
    return pl.pallas_call(
        functools.partial(_ffn_kernel, tiles_per_seq=seq // ROW_TILE, d_ff=d_ff),
        grid=(t // ROW_TILE,),
        in_specs=[_row_spec(d), _resident_layer(w_up.shape, layer), _resident(conv_w.shape),
                  _resident(cb.shape), _resident_layer(w_down.shape, layer), _resident((1, d)), _resident((1, d))],
        out_specs=[_row_spec(d), _row_spec(d)][:n_out],
        out_shape=[jax.ShapeDtypeStruct((t, d), F32), jax.ShapeDtypeStruct((t, d), BF16)][:n_out],
        scratch_shapes=[pltpu.VMEM((CONV_HALO, 2 * d_ff), F32),
                        pltpu.VMEM((4, ROW_TILE + CONV_HALO, COL_CHUNK), F32),
                        pltpu.VMEM((ROW_TILE, d_ff), BF16),
                        pltpu.VMEM((ROW_TILE, d), BF16)],
        compiler_params=_params(("arbitrary",)),
        name=name,
    )(x32, w_up, conv_w, cb, w_down, g.reshape(1, d), b.reshape(1, d))


def _ssd_kernel(xs_ref, bm_ref, cm_ref, dt_ref, z_ref, alog_ref, dskip_ref, ng_ref, o_ref, state):
    q = SSM_CHUNK

    @pl.when(pl.program_id(1) == 0)
    def _():
        state[...] = jnp.zeros(state.shape, F32)

    for sub in range(SSD_CHUNKS_PER_STEP):
        _ssd_chunk(slice(sub * q, (sub + 1) * q), xs_ref, bm_ref, cm_ref, dt_ref, z_ref, alog_ref, dskip_ref,
                   ng_ref, o_ref, state)


def _ssd_chunk(rows, xs_ref, bm_ref, cm_ref, dt_ref, z_ref, alog_ref, dskip_ref, ng_ref, o_ref, state):
    q = SSM_CHUNK
    gw = (SSM_HEADS // SSM_GROUPS) * SSM_HEADDIM
    pairs_per_group = gw // LANES
    lane_row = lax.broadcasted_iota(jnp.int32, (1, LANES), 1)
    first_head = lane_row < SSM_HEADDIM
    dt = dt_ref[rows, :]
    a_row = jnp.where(lane_row < SSM_HEADS, -jnp.exp(alog_ref[...]), 0.0)
    acs = _cumsum_rows(dt * a_row)
    acs_t = acs.T
    row = lax.broadcasted_iota(jnp.int32, (q, q), 0)
    col = lax.broadcasted_iota(jnp.int32, (q, q), 1)
    causal = row >= col

    inner = SSM_HEADS * SSM_HEADDIM
    assert SSM_HEADDIM & (SSM_HEADDIM - 1) == 0
    head_of_lane = lax.broadcasted_iota(jnp.int32, (LANES, inner), 1) >> (SSM_HEADDIM.bit_length() - 1)
    replicate = jnp.where(head_of_lane == lax.broadcasted_iota(jnp.int32, (LANES, inner), 0), 1.0, 0.0).astype(BF16)
    acs_x = None
    for part in _split3(acs):
        term = _mm(part, replicate)
        acs_x = term if acs_x is None else acs_x + term
    tot_x = acs_x[q - 1:q, :]
    grow_x = jnp.exp(acs_x)
    dend_x = jnp.exp(tot_x - acs_x)
    dec_x = jnp.exp(tot_x)

    groups = []
    for g in range(SSM_GROUPS):
        bmat = bm_ref[rows, g * SSM_STATE:(g + 1) * SSM_STATE]
        cmat = cm_ref[rows, g * SSM_STATE:(g + 1) * SSM_STATE]
        cb = lax.dot_general(cmat, bmat, (((1,), (1,)), ((), ())), preferred_element_type=F32)
        prev_t = state[g]
        y_off = _mm(cmat, prev_t.astype(BF16))
        groups.append((bmat, cb, prev_t, y_off))

    for g, (bmat, cb, prev_t, y_off) in enumerate(groups):
        gcols = slice(g * gw, (g + 1) * gw)
        xd_parts, u_parts = [], []
        for p in range(pairs_per_group):
            c0 = g * gw + p * LANES
            h0 = c0 // SSM_HEADDIM
            pcols = slice(c0, c0 + LANES)
            x2 = xs_ref[rows, pcols].astype(F32)
            dt2 = jnp.where(first_head, dt[:, h0:h0 + 1], dt[:, h0 + 1:h0 + 2])
            xdt2 = x2 * dt2
            xdt2_b = xdt2.astype(BF16)
            ydiag = []
            for e in (0, 1):
                seg = jnp.exp(jnp.where(causal, acs[:, h0 + e:h0 + e + 1] - acs_t[h0 + e:h0 + e + 1, :], -jnp.inf))
                ydiag.append(_mm((cb * seg).astype(BF16), xdt2_b))
            y2 = jnp.where(first_head, ydiag[0], ydiag[1]) + y_off[:, p * LANES:(p + 1) * LANES] * grow_x[:, pcols]
            xd_parts.append((xdt2 * dend_x[:, pcols]).astype(BF16))
            y2 = y2 + dskip_ref[:, pcols] * x2
            zz = z_ref[rows, pcols].astype(F32)
            u_parts.append(y2 * (zz * _sigmoid(zz)))
        xd = jnp.concatenate(xd_parts, axis=1)
        new_t = lax.dot_general(bmat, xd, (((0,), (0,)), ((), ())), preferred_element_type=F32)
        state[g] = prev_t * dec_x[:, gcols] + new_t
        u = jnp.concatenate(u_parts, axis=1)
        ms = jnp.mean(u * u, axis=-1, keepdims=True)
        o_ref[rows, gcols] = (u * lax.rsqrt(ms + LN_EPS) * ng_ref[:, gcols]).astype(o_ref.dtype)


def _ssd_call(xs, bm, cm, dt, z, alog_row, dskip_row, ng_row, *, batch, seq):
    t, inner = xs.shape
    q = SSM_CHUNK * SSD_CHUNKS_PER_STEP
    nc = seq // q
    gw = inner // SSM_GROUPS
    rows = lambda b, c: (b * nc + c, 0)
    const = lambda b, c: (0, 0)
    return pl.pallas_call(
        _ssd_kernel,
        grid=(batch, nc),
        in_specs=[pl.BlockSpec((q, inner), rows),
                  pl.BlockSpec((q, bm.shape[1]), rows),
                  pl.BlockSpec((q, cm.shape[1]), rows),
                  pl.BlockSpec((q, LANES), rows),
                  pl.BlockSpec((q, inner), rows),
                  pl.BlockSpec((1, LANES), const),
                  pl.BlockSpec((1, inner), const),
                  pl.BlockSpec((1, inner), const)],
        out_specs=pl.BlockSpec((q, inner), rows),
        out_shape=jax.ShapeDtypeStruct((t, inner), BF16),
        scratch_shapes=[pltpu.VMEM((SSM_GROUPS, SSM_STATE, gw), F32)],
        compiler_params=_params(("arbitrary", "arbitrary")),
        name="ssd_scan",
    )(xs, bm, cm, dt, z, alog_row, dskip_row, ng_row)


def _split3(x):
    hi = x.astype(BF16)
    r1 = x - hi.astype(F32)
    mid = r1.astype(BF16)
    lo = (r1 - mid.astype(F32)).astype(BF16)
    return hi, mid, lo


def _l1_in_kernel(x_ref, w_ref, wf_ref, bf_ref, qkv_ref, eq_ref, ek_ref, carry, *, tiles_per_seq, q_width):
    seq_start = pl.program_id(0) % tiles_per_seq == 0
    items = []
    for cols in _chunks(qkv_ref.shape[1]):
        def store(y, cols=cols):
            if cols.stop <= q_width:
                y = y * (FOX_HEADDIM ** -0.5 * LOG2E)
            qkv_ref[:, cols] = y.astype(qkv_ref.dtype)
        items.append((lambda cols=cols: _mm(x_ref[...], w_ref[:, cols]), store))

    def gate_scan(f_raw):
        f = f_raw + bf_ref[...]
        cum = _cumsum_rows(-_softplus(-f)) + jnp.where(seq_start, 0.0, carry[...])
        carry[...] = cum[cum.shape[0] - 1:, :]
        parts = _split3(cum * LOG2E)
        r = lax.broadcasted_iota(jnp.int32, (LANES, LANES), 0)
        l = lax.broadcasted_iota(jnp.int32, (LANES, LANES), 1)
        lane = lax.broadcasted_iota(jnp.int32, (1, LANES), 1) & (FOX_EXT - 1)
        spread_q = spread_k = None
        for c, part in enumerate(parts):
            sq = _mm(part, jnp.where(l == FOX_EXT * r + c, 1.0, 0.0).astype(BF16))
            sk = _mm(part, jnp.where(l == FOX_EXT * r + 3 + c, 1.0, 0.0).astype(BF16))
            spread_q = sq if spread_q is None else spread_q + sq
            spread_k = sk if spread_k is None else spread_k + sk
        ones_q = jnp.where((lane >= 3) & (lane < 6), 1.0, 0.0)
        ones_k = jnp.where(lane < 3, 1.0, 0.0)
        eq_ref[...] = (spread_q + ones_q).astype(BF16)
        ek_ref[...] = (ones_k - spread_k).astype(BF16)

    _issue_ahead(items + [(lambda: _mm(x_ref[...], wf_ref[...]), gate_scan)])


def _l1_in_call(xb, w_qkv, w_f, b_f, *, seq):
    t, d = xb.shape
    n = w_qkv.shape[1]
    return pl.pallas_call(
        functools.partial(_l1_in_kernel, tiles_per_seq=seq // ROW_TILE, q_width=n // 3),
        grid=(t // ROW_TILE,),
        in_specs=[_row_spec(d), _resident(w_qkv.shape), _resident(w_f.shape), _resident(b_f.shape)],
        out_specs=[_row_spec(n), _row_spec(LANES), _row_spec(LANES)],
        out_shape=[jax.ShapeDtypeStruct((t, n), BF16), jax.ShapeDtypeStruct((t, LANES), BF16),
                   jax.ShapeDtypeStruct((t, LANES), BF16)],
        scratch_shapes=[pltpu.VMEM((1, LANES), F32)],
        compiler_params=_params(("arbitrary",)),
        name="l1_in",
    )(xb, w_qkv, w_f, b_f)


def _fox_attn_kernel(q_ref, k_ref, v_ref, eq_ref, ek_ref, o_ref, vt_ref, *, tq, nq):
    hp = pl.program_id(1)
    lane2 = lax.broadcasted_iota(jnp.int32, (1, 2 * LANES), 1)
    head_lanes = []
    for e in (0, 1):
        ext0 = LANES + FOX_EXT * (2 * hp + e)
        head_lanes.append(((lane2 >= FOX_HEADDIM * e) & (lane2 < FOX_HEADDIM * (e + 1)))
                          | ((lane2 >= ext0) & (lane2 < ext0 + FOX_EXT)))
    key = lax.broadcasted_iota(jnp.int32, (tq, tq), 0)
    qry = lax.broadcasted_iota(jnp.int32, (tq, tq), 1)
    visible = key <= qry

    vrows = FOX_HEADDIM + FOX_ONES
    for j in range(nq):
        v_t = v_ref[j * tq:(j + 1) * tq, :].astype(F32).T.astype(BF16)
        for e in (0, 1):
            vt_ref[j, e, 0:FOX_HEADDIM, :] = v_t[FOX_HEADDIM * e:FOX_HEADDIM * (e + 1), :]
            vt_ref[j, e, FOX_HEADDIM:vrows, :] = jnp.ones((FOX_ONES, tq), BF16)

    def head_queries(i):
        rows = slice(i * tq, (i + 1) * tq)
        qe = jnp.concatenate([q_ref[rows, :], eq_ref[rows, :]], axis=1)
        return [jnp.where(m, qe, jnp.zeros_like(qe)) for m in head_lanes]

    def logits(qm, j):
        rows = slice(j * tq, (j + 1) * tq)
        ke = jnp.concatenate([k_ref[rows, :], ek_ref[rows, :]], axis=1)
        return lax.dot_general(ke, qm, (((1,), (1,)), ((), ())), preferred_element_type=F32)

    def softmax(s, m, diagonal):
        if diagonal:
            s = jnp.where(visible, s, NEG_BIG)
        m_new = jnp.maximum(m, jnp.max(s, axis=0, keepdims=True))
        return m_new, jnp.exp2(m - m_new), jnp.exp2(s - m_new).astype(BF16)

    steps = [(i, j) for i in range(nq) for j in range(i + 1)]
    queries = {}

    def masked_queries(i):
        if i not in queries:
            queries[i] = head_queries(i)
        return queries[i]

    outs = {}

    def finish(i, e, acc):
        outs.setdefault(i, {})[e] = acc[0:FOX_HEADDIM] / acc[FOX_HEADDIM:FOX_HEADDIM + 1]
        if len(outs[i]) == 2:
            o_t = jnp.concatenate([outs[i][0], outs[i][1]], axis=0)
            o_ref[i * tq:(i + 1) * tq, :] = o_t.T.astype(o_ref.dtype)
            del outs[i]

    class Stream:
        def __init__(self, e, steps):
            self.e = e
            self.steps = steps
            self.s_next = logits(masked_queries(steps[0][0])[e], steps[0][1])
            self.pending = None
            self.acc = {}
            self.m = None

        def retire(self, pending, pv):
            pi, pj, _, alpha = pending
            self.acc[pi] = pv if pj == 0 else alpha * self.acc[pi] + pv
            if pj == pi:
                finish(pi, self.e, self.acc.pop(pi))

        def step(self, k):
            if k >= len(self.steps):
                return
            i, j = self.steps[k]
            s = self.s_next
            if k + 1 < len(self.steps):
                self.s_next = logits(masked_queries(self.steps[k + 1][0])[self.e], self.steps[k + 1][1])
            pv = None if self.pending is None else _mm(vt_ref[self.pending[1], self.e], self.pending[2])
            if j == 0:
                self.m = jnp.full((1, tq), NEG_BIG, F32)
            self.m, alpha, p = softmax(s, self.m, j == i)
            if self.pending is not None:
                self.retire(self.pending, pv)
            self.pending = (i, j, p, alpha)

        def drain(self):
            self.retire(self.pending, _mm(vt_ref[self.pending[1], self.e], self.pending[2]))

    group = lambda i: (i % 4) in (0, 3)
    assert nq % 4 == 0
    lo = [st for st in steps if group(st[0])]
    hi = [st for st in steps if not group(st[0])]
    streams = [Stream(0, lo), Stream(1, lo), Stream(0, hi), Stream(1, hi)]
    for k in range(max(len(stream.steps) for stream in streams)):
        for stream in streams:
            stream.step(k)
    for stream in streams:
        stream.drain()


def _fox_attn_call(qkv, eq, ek, *, batch, seq, tq):
    t = qkv.shape[0]
    width = FOX_HEADS * FOX_HEADDIM
    npairs = width // LANES
    blk = lambda off: pl.BlockSpec((seq, LANES), lambda b, hp, o=off: (b, o + hp))
    ext = pl.BlockSpec((seq, LANES), lambda b, hp: (b, 0))
    return pl.pallas_call(
        functools.partial(_fox_attn_kernel, tq=tq, nq=seq // tq),
        grid=(batch, npairs),
        in_specs=[blk(0), blk(npairs), blk(2 * npairs), ext, ext],
        out_specs=pl.BlockSpec((seq, LANES), lambda b, hp: (b, hp)),
        out_shape=jax.ShapeDtypeStruct((t, width), BF16),
        scratch_shapes=[pltpu.VMEM((seq // tq, 2, FOX_HEADDIM + FOX_ONES, tq), BF16)],
        compiler_params=_params(("arbitrary", "arbitrary")),
        name="fox_attn",
    )(qkv, qkv, qkv, eq, ek)


def _pad_cols(a, width):
    return jnp.pad(a, ((0, 0), (0, width - a.shape[1])))


def kernel(x, sc_ssm_w_in, sc_conv_w, ssm_conv_w, ssm_conv_b, ssm_dt_bias, ssm_a_log, ssm_d, ssm_norm_g,
           sc_ssm_w_out, fox_w_in, fox_b_f, fox_w_out, ffn_w_up, ffn_conv_w, ffn_conv_b, ffn_w_down,
           ln_mix_g, ln_mix_b, ln_ffn_g, ln_ffn_b):
    batch, seq, d = x.shape
    t = batch * seq
    x32 = x.reshape(t, d)
    w_up, w_down = ffn_w_up.astype(BF16), ffn_w_down.astype(BF16)

    inner = SSM_HEADS * SSM_HEADDIM
    bc_w = SSM_GROUPS * SSM_STATE
    o_dt = 3 * d + 2 * inner + 2 * bc_w
    ya, z, xs, bm, cm, dt = _l0_in_call(
        x32, sc_ssm_w_in[0].astype(BF16), _pad_cols(sc_ssm_w_in[0][:, o_dt:], LANES).astype(BF16),
        sc_conv_w[0], ssm_conv_w[0], ssm_conv_b[0].reshape(1, -1), _pad_cols(ssm_dt_bias[0].reshape(1, -1), LANES),
        seq=seq, inner=inner, bc_w=bc_w)
    yb = _ssd_call(xs, bm, cm, dt, z,
                   _pad_cols(ssm_a_log[0].reshape(1, -1), LANES),
                   jnp.repeat(ssm_d[0], SSM_HEADDIM).reshape(1, -1),
                   ssm_norm_g[0].reshape(1, -1), batch=batch, seq=seq)
    x32 = _outproj_ln([ya, yb], sc_ssm_w_out[0].astype(BF16), x32, ln_mix_g[0], ln_mix_b[0], name="l0_out")
    x32, xb = _conv_ffn(x32, w_up, ffn_conv_w[0], ffn_conv_b[0], w_down, ln_ffn_g[0], ln_ffn_b[0],
                        layer=0, seq=seq, name="l0_ffn", emit_bf16=True)

    width = FOX_HEADS * FOX_HEADDIM
    fw = fox_w_in[0]
    qkv, eq, ek = _l1_in_call(xb, fw[:, :3 * width].astype(BF16), _pad_cols(fw[:, 3 * width:], LANES).astype(BF16),
                              _pad_cols(fox_b_f[0].reshape(1, -1), LANES), seq=seq)
    o = _fox_attn_call(qkv, eq, ek, batch=batch, seq=seq, tq=256)
    x32 = _outproj_ln([o], fox_w_out[0].astype(BF16), x32, ln_mix_g[1], ln_mix_b[1], name="l1_out")
    (x32,) = _conv_ffn(x32, w_up, ffn_conv_w[1], ffn_conv_b[1], w_down, ln_ffn_g[1], ln_ffn_b[1],
                       layer=1, seq=seq, name="l1_ffn", emit_bf16=False)
    return x32.reshape(batch, seq, d)
```

```python
import functools

import jax
import jax.numpy as jnp
from jax import lax
from jax.experimental import pallas as pl
from jax.experimental.pallas import tpu as pltpu

F32 = jnp.float32
BF16 = jnp.bfloat16

DEPTH = 2
ALPHA = (2 * DEPTH) ** 0.25
LN_EPS = 1e-5

SSM_HEADDIM = 64
SSM_HEADS = 16
SSM_GROUPS = 2
SSM_STATE = 128
SSM_CHUNK = 128
SSD_CHUNKS_PER_STEP = 4

FOX_HEADS = 16
FOX_HEADDIM = 64
FOX_EXT = 8
FOX_ONES = 16
LOG2E = 1.4426950408889634

LANES = 128
ROW_TILE = 512
PROJ_ROW_TILE = 1024
COL_CHUNK = 256
CONV_HALO = 8
NEG_BIG = -1e30
VMEM_LIMIT = 56 * 1024 * 1024


def _params(sem):
    return pltpu.CompilerParams(dimension_semantics=sem, vmem_limit_bytes=VMEM_LIMIT)


def _mm(a, b):
    return jnp.dot(a, b, preferred_element_type=F32)


def _sigmoid(x):
    return 1.0 / (1.0 + jnp.exp(-x))


def _softplus(x):
    return jnp.maximum(x, 0.0) + jnp.log1p(jnp.exp(-jnp.abs(x)))


def _cumsum_rows(x):
    n = x.shape[0]
    row = lax.broadcasted_iota(jnp.int32, x.shape, 0)
    s = 1
    while s < n:
        x = x + jnp.where(row >= s, pltpu.roll(x, s, 0), 0.0)
        s *= 2
    return x


def _chunks(width):
    assert width % COL_CHUNK == 0
    return [slice(c, c + COL_CHUNK) for c in range(0, width, COL_CHUNK)]


def _shift(cols, off):
    return slice(cols.start + off, cols.stop + off)


def _causal_conv(val, halo, cbuf, slot, cols, w, taps, seq_start):
    tm = val.shape[0]
    cbuf[slot, 0:CONV_HALO, :] = jnp.where(seq_start, 0.0, halo[:, cols])
    cbuf[slot, CONV_HALO:CONV_HALO + tm, :] = val
    halo[:, cols] = val[tm - CONV_HALO:tm, :]
    out = val * w[taps - 1:taps, :]
    for k in range(taps - 1):
        sh = taps - 1 - k
        out = out + cbuf[slot, CONV_HALO - sh:CONV_HALO - sh + tm, :] * w[k:k + 1, :]
    return out


def _issue_ahead(items):
    ahead = items[0][0]()
    for k, (_, finish) in enumerate(items):
        raw = ahead
        if k + 1 < len(items):
            ahead = items[k + 1][0]()
        finish(raw)


def _layer_norm(y, g, b):
    mu = jnp.mean(y, axis=-1, keepdims=True)
    d = y - mu
    var = jnp.mean(d * d, axis=-1, keepdims=True)
    return d * lax.rsqrt(var + LN_EPS) * g + b


def _resident(shape):
    return pl.BlockSpec(shape, lambda i: (0,) * len(shape), pipeline_mode=pl.Buffered(1))


def _resident_layer(shape, layer):
    return pl.BlockSpec((None,) + tuple(shape[1:]), lambda i: (layer,) + (0,) * (len(shape) - 1),
                        pipeline_mode=pl.Buffered(1))


def _row_spec(cols, tm=ROW_TILE):
    return pl.BlockSpec((tm, cols), lambda i: (i, 0))


def _l0_in_kernel(x_ref, w_ref, wdt_ref, scw_ref, xcw_ref, xcb_ref, dtb_ref,
                  ya_ref, z_ref, xs_ref, bm_ref, cm_ref, dt_ref,
                  halo_sc, halo_x, cbuf, xb_ref, *, tiles_per_seq, d, inner, bc_w):
    seq_start = pl.program_id(0) % tiles_per_seq == 0
    xb_ref[...] = x_ref[...].astype(BF16)

    def proj(off, cols):
        return _mm(xb_ref[...], w_ref[:, _shift(cols, off)])

    items = []
    for n, cols in enumerate(_chunks(d)):
        def sconv_done(raw, n=n, cols=cols):
            gate, c, h = raw
            y = _causal_conv(c * h, halo_sc, cbuf, n % 2, cols, scw_ref[:, cols], scw_ref.shape[0], seq_start)
            ya_ref[:, cols] = (gate * y).astype(ya_ref.dtype)
        items.append((lambda cols=cols: (proj(0, cols), proj(d, cols), proj(2 * d, cols)), sconv_done))
    for cols in _chunks(inner):
        def z_done(raw, cols=cols):
            z_ref[:, cols] = raw.astype(z_ref.dtype)
        items.append((lambda cols=cols: proj(3 * d, cols), z_done))
    for n, cols in enumerate(_chunks(inner + 2 * bc_w)):
        def xbc_done(raw, n=n, cols=cols):
            y = _causal_conv(raw, halo_x, cbuf, n % 2, cols, xcw_ref[:, cols], xcw_ref.shape[0], seq_start) + xcb_ref[:, cols]
            y = y * _sigmoid(y)
            if cols.stop <= inner:
                xs_ref[:, cols] = y.astype(xs_ref.dtype)
            elif cols.stop <= inner + bc_w:
                bm_ref[:, _shift(cols, -inner)] = y.astype(bm_ref.dtype)
            else:
                cm_ref[:, _shift(cols, -inner - bc_w)] = y.astype(cm_ref.dtype)
        items.append((lambda cols=cols: proj(3 * d + inner, cols), xbc_done))

    def dt_done(raw):
        dt_ref[...] = _softplus(raw + dtb_ref[...])
    items.append((lambda: _mm(xb_ref[...], wdt_ref[...]), dt_done))
    _issue_ahead(items)


def _l0_in_call(x32, w_in, w_dt, sc_cw, x_cw, x_cb, dt_b, *, seq, inner, bc_w):
    t, d = x32.shape
    assert bc_w == COL_CHUNK
    tm = PROJ_ROW_TILE
    kern = functools.partial(_l0_in_kernel, tiles_per_seq=seq // tm, d=d, inner=inner, bc_w=bc_w)
    return pl.pallas_call(
        kern,
        grid=(t // tm,),
        in_specs=[_row_spec(d, tm), _resident(w_in.shape), _resident(w_dt.shape), _resident(sc_cw.shape),
                  _resident(x_cw.shape), _resident(x_cb.shape), _resident(dt_b.shape)],
        out_specs=[_row_spec(d, tm), _row_spec(inner, tm), _row_spec(inner, tm), _row_spec(bc_w, tm),
                   _row_spec(bc_w, tm), _row_spec(LANES, tm)],
        out_shape=[jax.ShapeDtypeStruct((t, d), BF16), jax.ShapeDtypeStruct((t, inner), F32),
                   jax.ShapeDtypeStruct((t, inner), F32), jax.ShapeDtypeStruct((t, bc_w), BF16),
                   jax.ShapeDtypeStruct((t, bc_w), BF16), jax.ShapeDtypeStruct((t, LANES), F32)],
        scratch_shapes=[pltpu.VMEM((CONV_HALO, d), F32), pltpu.VMEM((CONV_HALO, inner + 2 * bc_w), F32),
                        pltpu.VMEM((2, tm + CONV_HALO, COL_CHUNK), F32),
                        pltpu.VMEM((tm, d), BF16)],
        compiler_params=_params(("arbitrary",)),
        name="l0_in",
    )(x32, w_in, w_dt, sc_cw, x_cw, x_cb, dt_b)


def _outproj_ln_kernel(*refs, n_in):
    a_refs = refs[:n_in]
    w_refs = refs[n_in:2 * n_in]
    xres_ref, g_ref, b_ref, o32_ref = refs[2 * n_in:]
    tm = xres_ref.shape[0]
    n_sub = 2

    def project(rows):
        acc = _mm(a_refs[0][rows, :], w_refs[0][...])
        for a_ref, w_ref in zip(a_refs[1:], w_refs[1:]):
            acc = acc + _mm(a_ref[rows, :], w_ref[...])
        return acc

    def normalize(acc, rows):
        out = _layer_norm(ALPHA * xres_ref[rows, :] + acc, g_ref[...], b_ref[...])
        o32_ref[rows, :] = out

    items = []
    for r in range(n_sub):
        rows = slice(r * tm // n_sub, (r + 1) * tm // n_sub)
        items.append((functools.partial(project, rows), functools.partial(normalize, rows=rows)))
    _issue_ahead(items)


def _outproj_ln(acts, w, xres, g, b, *, name):
    t, d = xres.shape
    width = acts[0].shape[1]
    assert all(a.shape[1] == width for a in acts) and w.shape[0] == width * len(acts)
    tm = PROJ_ROW_TILE
    in_specs = [_row_spec(width, tm) for _ in acts]
    in_specs += [pl.BlockSpec((width, d), lambda i, k=k: (k, 0), pipeline_mode=pl.Buffered(1)) for k in range(len(acts))]
    in_specs += [_row_spec(d, tm), _resident((1, d)), _resident((1, d))]
    weights = [w] * len(acts)
    return pl.pallas_call(
        functools.partial(_outproj_ln_kernel, n_in=len(acts)),
        grid=(t // tm,),
        in_specs=in_specs,
        out_specs=_row_spec(d, tm),
        out_shape=jax.ShapeDtypeStruct((t, d), F32),
        compiler_params=_params(("arbitrary",)),
        name=name,
    )(*acts, *weights, xres, g.reshape(1, d), b.reshape(1, d))


def _ffn_kernel(xres_ref, wup_ref, cw_ref, cb_ref, wd_ref, g_ref, b_ref, *rest, tiles_per_seq, d_ff):
    (o32_ref, *maybe_o16), (halo, cbuf, act_ref, x_ref) = rest[:-4], rest[-4:]
    seq_start = pl.program_id(0) % tiles_per_seq == 0
    taps = cw_ref.shape[0]
    chunks = _chunks(d_ff)
    x_ref[...] = xres_ref[...].astype(BF16)

    items = []
    for n, cols in enumerate(chunks):
        def gate(raw, n=n, cols=cols):
            gcols = _shift(cols, d_ff)
            u = _causal_conv(raw[0], halo, cbuf, 2 * (n % 2), cols, cw_ref[:, cols], taps, seq_start) + cb_ref[:, cols]
            gt = _causal_conv(raw[1], halo, cbuf, 2 * (n % 2) + 1, gcols, cw_ref[:, gcols], taps, seq_start) + cb_ref[:, gcols]
            act_ref[:, cols] = (u * (gt * _sigmoid(gt))).astype(BF16)
        items.append((lambda cols=cols: (_mm(x_ref[...], wup_ref[:, cols]),
                                         _mm(x_ref[...], wup_ref[:, _shift(cols, d_ff)])), gate))
    _issue_ahead(items)
    out = _layer_norm(ALPHA * xres_ref[...] + _mm(act_ref[...], wd_ref[...]), g_ref[...], b_ref[...])
    o32_ref[...] = out
    for o16_ref in maybe_o16:
        o16_ref[...] = out.astype(BF16)


def _conv_ffn(x32, w_up, conv_w, conv_b, w_down, g, b, *, layer, seq, name, emit_bf16):
    t, d = x32.shape
    d_ff = w_down.shape[1]
    cb = conv_b.reshape(1, -1)
    n_out = 2 if emit_bf16 else 1
    return pl.pallas_call(
        functools.partial(_ffn_kernel, tiles_per_seq=seq // ROW_TILE, d_ff=d_ff),
        grid=(t // ROW_TILE,),
        in_specs=[_row_spec(d), _resident_layer(w_up.shape, layer), _resident(conv_w.shape),
                  _resident(cb.shape), _resident_layer(w_down.shape, layer), _resident((1, d)), _resident((1, d))],
        out_specs=[_row_spec(d), _row_spec(d)][:n_out],
        out_shape=[jax.ShapeDtypeStruct((t, d), F32), jax.ShapeDtypeStruct((t, d), BF16)][:n_out],
        scratch_shapes=[pltpu.VMEM((CONV_HALO, 2 * d_ff), F32),
                        pltpu.VMEM((4, ROW_TILE + CONV_HALO, COL_CHUNK), F32),
                        pltpu.VMEM((ROW_TILE, d_ff), BF16),
                        pltpu.VMEM((ROW_TILE, d), BF16)],
        compiler_params=_params(("arbitrary",)),
        name=name,
    )(x32, w_up, conv_w, cb, w_down, g.reshape(1, d), b.reshape(1, d))


def _ssd_kernel(xs_ref, bm_ref, cm_ref, dt_ref, z_ref, alog_ref, dskip_ref, ng_ref, o_ref, state):
    q = SSM_CHUNK

    @pl.when(pl.program_id(1) == 0)
    def _():
        state[...] = jnp.zeros(state.shape, F32)

    for sub in range(SSD_CHUNKS_PER_STEP):
        _ssd_chunk(slice(sub * q, (sub + 1) * q), xs_ref, bm_ref, cm_ref, dt_ref, z_ref, alog_ref, dskip_ref,
                   ng_ref, o_ref, state)


def _ssd_chunk(rows, xs_ref, bm_ref, cm_ref, dt_ref, z_ref, alog_ref, dskip_ref, ng_ref, o_ref, state):
    q = SSM_CHUNK
    gw = (SSM_HEADS // SSM_GROUPS) * SSM_HEADDIM
    pairs_per_group = gw // LANES
    lane_row = lax.broadcasted_iota(jnp.int32, (1, LANES), 1)
    first_head = lane_row < SSM_HEADDIM
    dt = dt_ref[rows, :]
    a_row = jnp.where(lane_row < SSM_HEADS, -jnp.exp(alog_ref[...]), 0.0)
    acs = _cumsum_rows(dt * a_row)
    acs_t = acs.T
    row = lax.broadcasted_iota(jnp.int32, (q, q), 0)
    col = lax.broadcasted_iota(jnp.int32, (q, q), 1)
    causal = row >= col

    inner = SSM_HEADS * SSM_HEADDIM
    assert SSM_HEADDIM & (SSM_HEADDIM - 1) == 0
    head_of_lane = lax.broadcasted_iota(jnp.int32, (LANES, inner), 1) >> (SSM_HEADDIM.bit_length() - 1)
    replicate = jnp.where(head_of_lane == lax.broadcasted_iota(jnp.int32, (LANES, inner), 0), 1.0, 0.0).astype(BF16)
    acs_x = None
    for part in _split3(acs):
        term = _mm(part, replicate)
        acs_x = term if acs_x is None else acs_x + term
    tot_x = acs_x[q - 1:q, :]
    grow_x = jnp.exp(acs_x)
    dend_x = jnp.exp(tot_x - acs_x)
    dec_x = jnp.exp(tot_x)

    groups = []
    for g in range(SSM_GROUPS):
        bmat = bm_ref[rows, g * SSM_STATE:(g + 1) * SSM_STATE]
        cmat = cm_ref[rows, g * SSM_STATE:(g + 1) * SSM_STATE]
        cb = lax.dot_general(cmat, bmat, (((1,), (1,)), ((), ())), preferred_element_type=F32)
        prev_t = state[g]
        y_off = _mm(cmat, prev_t.astype(BF16))
        groups.append((bmat, cb, prev_t, y_off))

    for g, (bmat, cb, prev_t, y_off) in enumerate(groups):
        gcols = slice(g * gw, (g + 1) * gw)
        xd_parts, u_parts = [], []
        for p in range(pairs_per_group):
            c0 = g * gw + p * LANES
            h0 = c0 // SSM_HEADDIM
            pcols = slice(c0, c0 + LANES)
            x2 = xs_ref[rows, pcols]
            dt2 = jnp.where(first_head, dt[:, h0:h0 + 1], dt[:, h0 + 1:h0 + 2])
            xdt2 = x2 * dt2
            xdt2_b = xdt2.astype(BF16)
            ydiag = []
            for e in (0, 1):
                seg = jnp.exp(jnp.where(causal, acs[:, h0 + e:h0 + e + 1] - acs_t[h0 + e:h0 + e + 1, :], -jnp.inf))
                ydiag.append(_mm((cb * seg).astype(BF16), xdt2_b))
            y2 = jnp.where(first_head, ydiag[0], ydiag[1]) + y_off[:, p * LANES:(p + 1) * LANES] * grow_x[:, pcols]
            xd_parts.append((xdt2 * dend_x[:, pcols]).astype(BF16))
            y2 = y2 + dskip_ref[:, pcols] * x2
            zz = z_ref[rows, pcols]
            u_parts.append(y2 * (zz * _sigmoid(zz)))
        xd = jnp.concatenate(xd_parts, axis=1)
        new_t = lax.dot_general(bmat, xd, (((0,), (0,)), ((), ())), preferred_element_type=F32)
        state[g] = prev_t * dec_x[:, gcols] + new_t
        u = jnp.concatenate(u_parts, axis=1)
        ms = jnp.mean(u * u, axis=-1, keepdims=True)
        o_ref[rows, gcols] = (u * lax.rsqrt(ms + LN_EPS) * ng_ref[:, gcols]).astype(o_ref.dtype)


def _ssd_call(xs, bm, cm, dt, z, alog_row, dskip_row, ng_row, *, batch, seq):
    t, inner = xs.shape
    q = SSM_CHUNK * SSD_CHUNKS_PER_STEP
    nc = seq // q
    gw = inner // SSM_GROUPS
    rows = lambda b, c: (b * nc + c, 0)
    const = lambda b, c: (0, 0)
    return pl.pallas_call(
        _ssd_kernel,
        grid=(batch, nc),
        in_specs=[pl.BlockSpec((q, inner), rows),
                  pl.BlockSpec((q, bm.shape[1]), rows),
                  pl.BlockSpec((q, cm.shape[1]), rows),
                  pl.BlockSpec((q, LANES), rows),
                  pl.BlockSpec((q, inner), rows),
                  pl.BlockSpec((1, LANES), const),
                  pl.BlockSpec((1, inner), const),
                  pl.BlockSpec((1, inner), const)],
        out_specs=pl.BlockSpec((q, inner), rows),
        out_shape=jax.ShapeDtypeStruct((t, inner), BF16),
        scratch_shapes=[pltpu.VMEM((SSM_GROUPS, SSM_STATE, gw), F32)],
        compiler_params=_params(("arbitrary", "arbitrary")),
        name="ssd_scan",
    )(xs, bm, cm, dt, z, alog_row, dskip_row, ng_row)


def _split3(x):
    hi = x.astype(BF16)
    r1 = x - hi.astype(F32)
    mid = r1.astype(BF16)
    lo = (r1 - mid.astype(F32)).astype(BF16)
    return hi, mid, lo


def _l1_in_kernel(x_ref, w_ref, wf_ref, bf_ref, qkv_ref, eq_ref, ek_ref, carry, *, tiles_per_seq, q_width):
    seq_start = pl.program_id(0) % tiles_per_seq == 0
    items = []
    for cols in _chunks(qkv_ref.shape[1]):
        def store(y, cols=cols):
            if cols.stop <= q_width:
                y = y * (FOX_HEADDIM ** -0.5 * LOG2E)
            qkv_ref[:, cols] = y.astype(qkv_ref.dtype)
        items.append((lambda cols=cols: _mm(x_ref[...], w_ref[:, cols]), store))

    def gate_scan(f_raw):
        f = f_raw + bf_ref[...]
        cum = _cumsum_rows(-_softplus(-f)) + jnp.where(seq_start, 0.0, carry[...])
        carry[...] = cum[cum.shape[0] - 1:, :]
        parts = _split3(cum * LOG2E)
        r = lax.broadcasted_iota(jnp.int32, (LANES, LANES), 0)
        l = lax.broadcasted_iota(jnp.int32, (LANES, LANES), 1)
        lane = lax.broadcasted_iota(jnp.int32, (1, LANES), 1) & (FOX_EXT - 1)
        spread_q = spread_k = None
        for c, part in enumerate(parts):
            sq = _mm(part, jnp.where(l == FOX_EXT * r + c, 1.0, 0.0).astype(BF16))
            sk = _mm(part, jnp.where(l == FOX_EXT * r + 3 + c, 1.0, 0.0).astype(BF16))
            spread_q = sq if spread_q is None else spread_q + sq
            spread_k = sk if spread_k is None else spread_k + sk
        ones_q = jnp.where((lane >= 3) & (lane < 6), 1.0, 0.0)
        ones_k = jnp.where(lane < 3, 1.0, 0.0)
        eq_ref[...] = (spread_q + ones_q).astype(BF16)
        ek_ref[...] = (ones_k - spread_k).astype(BF16)

    _issue_ahead(items + [(lambda: _mm(x_ref[...], wf_ref[...]), gate_scan)])


def _l1_in_call(xb, w_qkv, w_f, b_f, *, seq):
    t, d = xb.shape
    n = w_qkv.shape[1]
    tm = PROJ_ROW_TILE
    return pl.pallas_call(
        functools.partial(_l1_in_kernel, tiles_per_seq=seq // tm, q_width=n // 3),
        grid=(t // tm,),
        in_specs=[_row_spec(d, tm), _resident(w_qkv.shape), _resident(w_f.shape), _resident(b_f.shape)],
        out_specs=[_row_spec(n, tm), _row_spec(LANES, tm), _row_spec(LANES, tm)],
        out_shape=[jax.ShapeDtypeStruct((t, n), BF16), jax.ShapeDtypeStruct((t, LANES), BF16),
                   jax.ShapeDtypeStruct((t, LANES), BF16)],
        scratch_shapes=[pltpu.VMEM((1, LANES), F32)],
        compiler_params=_params(("arbitrary",)),
        name="l1_in",
    )(xb, w_qkv, w_f, b_f)


def _fox_attn_kernel(q_ref, k_ref, v_ref, eq_ref, ek_ref, o_ref, vt_ref, *, tq, nq):
    hp = pl.program_id(1)
    lane2 = lax.broadcasted_iota(jnp.int32, (1, 2 * LANES), 1)
    head_lanes = []
    for e in (0, 1):
        ext0 = LANES + FOX_EXT * (2 * hp + e)
        head_lanes.append(((lane2 >= FOX_HEADDIM * e) & (lane2 < FOX_HEADDIM * (e + 1)))
                          | ((lane2 >= ext0) & (lane2 < ext0 + FOX_EXT)))
    key = lax.broadcasted_iota(jnp.int32, (tq, tq), 0)
    qry = lax.broadcasted_iota(jnp.int32, (tq, tq), 1)
    visible = key <= qry

    vrows = FOX_HEADDIM + FOX_ONES
    for j in range(nq):
        v_t = v_ref[j * tq:(j + 1) * tq, :].astype(F32).T.astype(BF16)
        for e in (0, 1):
            vt_ref[j, e, 0:FOX_HEADDIM, :] = v_t[FOX_HEADDIM * e:FOX_HEADDIM * (e + 1), :]
            vt_ref[j, e, FOX_HEADDIM:vrows, :] = jnp.ones((FOX_ONES, tq), BF16)

    def head_queries(i):
        rows = slice(i * tq, (i + 1) * tq)
        qe = jnp.concatenate([q_ref[rows, :], eq_ref[rows, :]], axis=1)
        return [jnp.where(m, qe, jnp.zeros_like(qe)) for m in head_lanes]

    def logits(qm, j):
        rows = slice(j * tq, (j + 1) * tq)
        ke = jnp.concatenate([k_ref[rows, :], ek_ref[rows, :]], axis=1)
        return lax.dot_general(ke, qm, (((1,), (1,)), ((), ())), preferred_element_type=F32)

    def softmax(s, m, diagonal):
        if diagonal:
            s = jnp.where(visible, s, NEG_BIG)
        m_new = jnp.maximum(m, jnp.max(s, axis=0, keepdims=True))
        return m_new, jnp.exp2(m - m_new), jnp.exp2(s - m_new).astype(BF16)

    steps = [(i, j) for i in range(nq) for j in range(i + 1)]
    queries = {}

    def masked_queries(i):
        if i not in queries:
            queries[i] = head_queries(i)
        return queries[i]

    outs = {}

    def finish(i, e, acc):
        outs.setdefault(i, {})[e] = acc[0:FOX_HEADDIM] / acc[FOX_HEADDIM:FOX_HEADDIM + 1]
        if len(outs[i]) == 2:
            o_t = jnp.concatenate([outs[i][0], outs[i][1]], axis=0)
            o_ref[i * tq:(i + 1) * tq, :] = o_t.T.astype(o_ref.dtype)
            del outs[i]

    class Stream:
        def __init__(self, e, steps):
            self.e = e
            self.steps = steps
            self.s_next = logits(masked_queries(steps[0][0])[e], steps[0][1])
            self.pending = None
            self.acc = {}
            self.m = None

        def retire(self, pending, pv):
            pi, pj, _, alpha = pending
            self.acc[pi] = pv if pj == 0 else alpha * self.acc[pi] + pv
            if pj == pi:
                finish(pi, self.e, self.acc.pop(pi))

        def step(self, k):
            if k >= len(self.steps):
                return
            i, j = self.steps[k]
            s = self.s_next
            if k + 1 < len(self.steps):
                self.s_next = logits(masked_queries(self.steps[k + 1][0])[self.e], self.steps[k + 1][1])
            pv = None if self.pending is None else _mm(vt_ref[self.pending[1], self.e], self.pending[2])
            if j == 0:
                self.m = jnp.full((1, tq), NEG_BIG, F32)
            self.m, alpha, p = softmax(s, self.m, j == i)
            if self.pending is not None:
                self.retire(self.pending, pv)
            self.pending = (i, j, p, alpha)

        def drain(self):
            self.retire(self.pending, _mm(vt_ref[self.pending[1], self.e], self.pending[2]))

    group = lambda i: (i % 4) in (0, 3)
    assert nq % 4 == 0
    lo = [st for st in steps if group(st[0])]
    hi = [st for st in steps if not group(st[0])]
    streams = [Stream(0, lo), Stream(1, lo), Stream(0, hi), Stream(1, hi)]
    for k in range(max(len(stream.steps) for stream in streams)):
        for stream in streams:
            stream.step(k)
    for stream in streams:
        stream.drain()


def _fox_attn_call(qkv, eq, ek, *, batch, seq, tq):
    t = qkv.shape[0]
    width = FOX_HEADS * FOX_HEADDIM
    npairs = width // LANES
    blk = lambda off: pl.BlockSpec((seq, LANES), lambda b, hp, o=off: (b, o + hp))
    ext = pl.BlockSpec((seq, LANES), lambda b, hp: (b, 0))
    return pl.pallas_call(
        functools.partial(_fox_attn_kernel, tq=tq, nq=seq // tq),
        grid=(batch, npairs),
        in_specs=[blk(0), blk(npairs), blk(2 * npairs), ext, ext],
        out_specs=pl.BlockSpec((seq, LANES), lambda b, hp: (b, hp)),
        out_shape=jax.ShapeDtypeStruct((t, width), BF16),
        scratch_shapes=[pltpu.VMEM((seq // tq, 2, FOX_HEADDIM + FOX_ONES, tq), BF16)],
        compiler_params=_params(("arbitrary", "arbitrary")),
        name="fox_attn",
    )(qkv, qkv, qkv, eq, ek)


def _pad_cols(a, width):
    return jnp.pad(a, ((0, 0), (0, width - a.shape[1])))


def kernel(x, sc_ssm_w_in, sc_conv_w, ssm_conv_w, ssm_conv_b, ssm_dt_bias, ssm_a_log, ssm_d, ssm_norm_g,
           sc_ssm_w_out, fox_w_in, fox_b_f, fox_w_out, ffn_w_up, ffn_conv_w, ffn_conv_b, ffn_w_down,
           ln_mix_g, ln_mix_b, ln_ffn_g, ln_ffn_b):
    batch, seq, d = x.shape
    t = batch * seq
    x32 = x.reshape(t, d)
    w_up, w_down = ffn_w_up.astype(BF16), ffn_w_down.astype(BF16)

    inner = SSM_HEADS * SSM_HEADDIM
    bc_w = SSM_GROUPS * SSM_STATE
    o_dt = 3 * d + 2 * inner + 2 * bc_w
    ya, z, xs, bm, cm, dt = _l0_in_call(
        x32, sc_ssm_w_in[0].astype(BF16), _pad_cols(sc_ssm_w_in[0][:, o_dt:], LANES).astype(BF16),
        sc_conv_w[0], ssm_conv_w[0], ssm_conv_b[0].reshape(1, -1), _pad_cols(ssm_dt_bias[0].reshape(1, -1), LANES),
        seq=seq, inner=inner, bc_w=bc_w)
    yb = _ssd_call(xs, bm, cm, dt, z,
                   _pad_cols(ssm_a_log[0].reshape(1, -1), LANES),
                   jnp.repeat(ssm_d[0], SSM_HEADDIM).reshape(1, -1),
                   ssm_norm_g[0].reshape(1, -1), batch=batch, seq=seq)
    x32 = _outproj_ln([ya, yb], sc_ssm_w_out[0].astype(BF16), x32, ln_mix_g[0], ln_mix_b[0], name="l0_out")
    x32, xb = _conv_ffn(x32, w_up, ffn_conv_w[0], ffn_conv_b[0], w_down, ln_ffn_g[0], ln_ffn_b[0],
                        layer=0, seq=seq, name="l0_ffn", emit_bf16=True)

    width = FOX_HEADS * FOX_HEADDIM
    fw = fox_w_in[0]
    qkv, eq, ek = _l1_in_call(xb, fw[:, :3 * width].astype(BF16), _pad_cols(fw[:, 3 * width:], LANES).astype(BF16),
                              _pad_cols(fox_b_f[0].reshape(1, -1), LANES), seq=seq)
    o = _fox_attn_call(qkv, eq, ek, batch=batch, seq=seq, tq=256)
    x32 = _outproj_ln([o], fox_w_out[0].astype(BF16), x32, ln_mix_g[1], ln_mix_b[1], name="l1_out")
    (x32,) = _conv_ffn(x32, w_up, ffn_conv_w[1], ffn_conv_b[1], w_down, ln_ffn_g[1], ln_ffn_b[1],
                       layer=1, seq=seq, name="l1_ffn", emit_bf16=False)
    return x32.reshape(batch, seq, d)
```

```python
import functools

import jax
import jax.numpy as jnp
from jax import lax
from jax.experimental import pallas as pl
from jax.experimental.pallas import tpu as pltpu

F32 = jnp.float32
BF16 = jnp.bfloat16

DEPTH = 2
ALPHA = (2 * DEPTH) ** 0.25
LN_EPS = 1e-5

SSM_HEADDIM = 64
SSM_HEADS = 16
SSM_GROUPS = 2
SSM_STATE = 128
SSM_CHUNK = 128
SSD_CHUNKS_PER_STEP = 4

FOX_HEADS = 16
FOX_HEADDIM = 64
FOX_EXT = 8
FOX_ONES = 16
FOX_PAIRS_PER_STEP = 2
LOG2E = 1.4426950408889634

LANES = 128
ROW_TILE = 512
PROJ_ROW_TILE = 1024
COL_CHUNK = 256
CONV_HALO = 8
NEG_BIG = -1e30
VMEM_LIMIT = 56 * 1024 * 1024


def _params(sem):
    return pltpu.CompilerParams(dimension_semantics=sem, vmem_limit_bytes=VMEM_LIMIT)


def _mm(a, b):
    return jnp.dot(a, b, preferred_element_type=F32)


def _sigmoid(x):
    return 1.0 / (1.0 + jnp.exp(-x))


def _softplus(x):
    return jnp.maximum(x, 0.0) + jnp.log1p(jnp.exp(-jnp.abs(x)))


def _cumsum_rows(x):
    n = x.shape[0]
    row = lax.broadcasted_iota(jnp.int32, x.shape, 0)
    s = 1
    while s < n:
        x = x + jnp.where(row >= s, pltpu.roll(x, s, 0), 0.0)
        s *= 2
    return x


def _chunks(width):
    assert width % COL_CHUNK == 0
    return [slice(c, c + COL_CHUNK) for c in range(0, width, COL_CHUNK)]


def _shift(cols, off):
    return slice(cols.start + off, cols.stop + off)


def _causal_conv(val, halo, cbuf, slot, cols, w, taps, seq_start):
    tm = val.shape[0]
    cbuf[slot, 0:CONV_HALO, :] = jnp.where(seq_start, 0.0, halo[:, cols])
    cbuf[slot, CONV_HALO:CONV_HALO + tm, :] = val
    halo[:, cols] = val[tm - CONV_HALO:tm, :]
    out = val * w[taps - 1:taps, :]
    for k in range(taps - 1):
        sh = taps - 1 - k
        out = out + cbuf[slot, CONV_HALO - sh:CONV_HALO - sh + tm, :] * w[k:k + 1, :]
    return out


def _issue_ahead(items):
    ahead = items[0][0]()
    for k, (_, finish) in enumerate(items):
        raw = ahead
        if k + 1 < len(items):
            ahead = items[k + 1][0]()
        finish(raw)


def _layer_norm(y, g, b):
    mu = jnp.mean(y, axis=-1, keepdims=True)
    d = y - mu
    var = jnp.mean(d * d, axis=-1, keepdims=True)
    return d * lax.rsqrt(var + LN_EPS) * g + b


def _resident(shape):
    return pl.BlockSpec(shape, lambda i: (0,) * len(shape), pipeline_mode=pl.Buffered(1))


def _resident_layer(shape, layer):
    return pl.BlockSpec((None,) + tuple(shape[1:]), lambda i: (layer,) + (0,) * (len(shape) - 1),
                        pipeline_mode=pl.Buffered(1))


def _row_spec(cols, tm=ROW_TILE):
    return pl.BlockSpec((tm, cols), lambda i: (i, 0))


def _l0_in_kernel(x_ref, w_ref, wdt_ref, scw_ref, xcw_ref, xcb_ref, dtb_ref,
                  ya_ref, z_ref, xs_ref, bm_ref, cm_ref, dt_ref,
                  halo_sc, halo_x, cbuf, xb_ref, *, tiles_per_seq, d, inner, bc_w):
    seq_start = pl.program_id(0) % tiles_per_seq == 0
    xb_ref[...] = x_ref[...].astype(BF16)

    def proj(off, cols):
        return _mm(xb_ref[...], w_ref[:, _shift(cols, off)])

    items = []
    for n, cols in enumerate(_chunks(d)):
        def sconv_done(raw, n=n, cols=cols):
            gate, c, h = raw
            y = _causal_conv(c * h, halo_sc, cbuf, n % 2, cols, scw_ref[:, cols], scw_ref.shape[0], seq_start)
            ya_ref[:, cols] = (gate * y).astype(ya_ref.dtype)
        items.append((lambda cols=cols: (proj(0, cols), proj(d, cols), proj(2 * d, cols)), sconv_done))
    for cols in _chunks(inner):
        def z_done(raw, cols=cols):
            z_ref[:, cols] = raw.astype(z_ref.dtype)
        items.append((lambda cols=cols: proj(3 * d, cols), z_done))
    for n, cols in enumerate(_chunks(inner + 2 * bc_w)):
        def xbc_done(raw, n=n, cols=cols):
            y = _causal_conv(raw, halo_x, cbuf, n % 2, cols, xcw_ref[:, cols], xcw_ref.shape[0], seq_start) + xcb_ref[:, cols]
            y = y * _sigmoid(y)
            if cols.stop <= inner:
                xs_ref[:, cols] = y.astype(xs_ref.dtype)
            elif cols.stop <= inner + bc_w:
                bm_ref[:, _shift(cols, -inner)] = y.astype(bm_ref.dtype)
            else:
                cm_ref[:, _shift(cols, -inner - bc_w)] = y.astype(cm_ref.dtype)
        items.append((lambda cols=cols: proj(3 * d + inner, cols), xbc_done))

    def dt_done(raw):
        dt_ref[...] = _softplus(raw + dtb_ref[...])
    items.append((lambda: _mm(xb_ref[...], wdt_ref[...]), dt_done))
    _issue_ahead(items)


def _l0_in_call(x32, w_in, w_dt, sc_cw, x_cw, x_cb, dt_b, *, seq, inner, bc_w):
    t, d = x32.shape
    assert bc_w == COL_CHUNK
    tm = PROJ_ROW_TILE
    kern = functools.partial(_l0_in_kernel, tiles_per_seq=seq // tm, d=d, inner=inner, bc_w=bc_w)
    return pl.pallas_call(
        kern,
        grid=(t // tm,),
        in_specs=[_row_spec(d, tm), _resident(w_in.shape), _resident(w_dt.shape), _resident(sc_cw.shape),
                  _resident(x_cw.shape), _resident(x_cb.shape), _resident(dt_b.shape)],
        out_specs=[_row_spec(d, tm), _row_spec(inner, tm), _row_spec(inner, tm), _row_spec(bc_w, tm),
                   _row_spec(bc_w, tm), _row_spec(LANES, tm)],
        out_shape=[jax.ShapeDtypeStruct((t, d), BF16), jax.ShapeDtypeStruct((t, inner), F32),
                   jax.ShapeDtypeStruct((t, inner), F32), jax.ShapeDtypeStruct((t, bc_w), BF16),
                   jax.ShapeDtypeStruct((t, bc_w), BF16), jax.ShapeDtypeStruct((t, LANES), F32)],
        scratch_shapes=[pltpu.VMEM((CONV_HALO, d), F32), pltpu.VMEM((CONV_HALO, inner + 2 * bc_w), F32),
                        pltpu.VMEM((2, tm + CONV_HALO, COL_CHUNK), F32),
                        pltpu.VMEM((tm, d), BF16)],
        compiler_params=_params(("arbitrary",)),
        name="l0_in",
    )(x32, w_in, w_dt, sc_cw, x_cw, x_cb, dt_b)


def _outproj_ln_kernel(*refs, n_in):
    a_refs = refs[:n_in]
    w_refs = refs[n_in:2 * n_in]
    xres_ref, g_ref, b_ref, o32_ref = refs[2 * n_in:]
    tm = xres_ref.shape[0]
    n_sub = 2

    def project(rows):
        acc = _mm(a_refs[0][rows, :], w_refs[0][...])
        for a_ref, w_ref in zip(a_refs[1:], w_refs[1:]):
            acc = acc + _mm(a_ref[rows, :], w_ref[...])
        return acc

    def normalize(acc, rows):
        out = _layer_norm(ALPHA * xres_ref[rows, :] + acc, g_ref[...], b_ref[...])
        o32_ref[rows, :] = out

    items = []
    for r in range(n_sub):
        rows = slice(r * tm // n_sub, (r + 1) * tm // n_sub)
        items.append((functools.partial(project, rows), functools.partial(normalize, rows=rows)))
    _issue_ahead(items)


def _outproj_ln(acts, w, xres, g, b, *, name):
    t, d = xres.shape
    width = acts[0].shape[1]
    assert all(a.shape[1] == width for a in acts) and w.shape[0] == width * len(acts)
    tm = PROJ_ROW_TILE
    in_specs = [_row_spec(width, tm) for _ in acts]
    in_specs += [pl.BlockSpec((width, d), lambda i, k=k: (k, 0), pipeline_mode=pl.Buffered(1)) for k in range(len(acts))]
    in_specs += [_row_spec(d, tm), _resident((1, d)), _resident((1, d))]
    weights = [w] * len(acts)
    return pl.pallas_call(
        functools.partial(_outproj_ln_kernel, n_in=len(acts)),
        grid=(t // tm,),
        in_specs=in_specs,
        out_specs=_row_spec(d, tm),
        out_shape=jax.ShapeDtypeStruct((t, d), F32),
        compiler_params=_params(("arbitrary",)),
        name=name,
    )(*acts, *weights, xres, g.reshape(1, d), b.reshape(1, d))


def _ffn_kernel(xres_ref, wup_ref, cw_ref, cb_ref, wd_ref, g_ref, b_ref, *rest, tiles_per_seq, d_ff):
    (o32_ref, *maybe_o16), (halo, cbuf, act_ref, x_ref) = rest[:-4], rest[-4:]
    seq_start = pl.program_id(0) % tiles_per_seq == 0
    taps = cw_ref.shape[0]
    chunks = _chunks(d_ff)
    x_ref[...] = xres_ref[...].astype(BF16)

    items = []
    for n, cols in enumerate(chunks):
        def gate(raw, n=n, cols=cols):
            gcols = _shift(cols, d_ff)
            u = _causal_conv(raw[0], halo, cbuf, 2 * (n % 2), cols, cw_ref[:, cols], taps, seq_start) + cb_ref[:, cols]
            gt = _causal_conv(raw[1], halo, cbuf, 2 * (n % 2) + 1, gcols, cw_ref[:, gcols], taps, seq_start) + cb_ref[:, gcols]
            act_ref[:, cols] = (u * (gt * _sigmoid(gt))).astype(BF16)
        items.append((lambda cols=cols: (_mm(x_ref[...], wup_ref[:, cols]),
                                         _mm(x_ref[...], wup_ref[:, _shift(cols, d_ff)])), gate))
    _issue_ahead(items)
    out = _layer_norm(ALPHA * xres_ref[...] + _mm(act_ref[...], wd_ref[...]), g_ref[...], b_ref[...])
    o32_ref[...] = out
    for o16_ref in maybe_o16:
        o16_ref[...] = out.astype(BF16)


def _conv_ffn(x32, w_up, conv_w, conv_b, w_down, g, b, *, layer, seq, name, emit_bf16):
    t, d = x32.shape
    d_ff = w_down.shape[1]
    cb = conv_b.reshape(1, -1)
    n_out = 2 if emit_bf16 else 1
    return pl.pallas_call(
        functools.partial(_ffn_kernel, tiles_per_seq=seq // ROW_TILE, d_ff=d_ff),
        grid=(t // ROW_TILE,),
        in_specs=[_row_spec(d), _resident_layer(w_up.shape, layer), _resident(conv_w.shape),
                  _resident(cb.shape), _resident_layer(w_down.shape, layer), _resident((1, d)), _resident((1, d))],
        out_specs=[_row_spec(d), _row_spec(d)][:n_out],
        out_shape=[jax.ShapeDtypeStruct((t, d), F32), jax.ShapeDtypeStruct((t, d), BF16)][:n_out],
        scratch_shapes=[pltpu.VMEM((CONV_HALO, 2 * d_ff), F32),
                        pltpu.VMEM((4, ROW_TILE + CONV_HALO, COL_CHUNK), F32),
                        pltpu.VMEM((ROW_TILE, d_ff), BF16),
                        pltpu.VMEM((ROW_TILE, d), BF16)],
        compiler_params=_params(("arbitrary",)),
        name=name,
    )(x32, w_up, conv_w, cb, w_down, g.reshape(1, d), b.reshape(1, d))


def _ssd_kernel(xs_ref, bm_ref, cm_ref, dt_ref, z_ref, alog_ref, dskip_ref, ng_ref, o_ref, state):
    q = SSM_CHUNK

    @pl.when(pl.program_id(1) == 0)
    def _():
        state[...] = jnp.zeros(state.shape, F32)

    for sub in range(SSD_CHUNKS_PER_STEP):
        _ssd_chunk(slice(sub * q, (sub + 1) * q), xs_ref, bm_ref, cm_ref, dt_ref, z_ref, alog_ref, dskip_ref,
                   ng_ref, o_ref, state)


def _ssd_chunk(rows, xs_ref, bm_ref, cm_ref, dt_ref, z_ref, alog_ref, dskip_ref, ng_ref, o_ref, state):
    q = SSM_CHUNK
    gw = (SSM_HEADS // SSM_GROUPS) * SSM_HEADDIM
    pairs_per_group = gw // LANES
    lane_row = lax.broadcasted_iota(jnp.int32, (1, LANES), 1)
    first_head = lane_row < SSM_HEADDIM
    dt = dt_ref[rows, :]
    a_row = jnp.where(lane_row < SSM_HEADS, -jnp.exp(alog_ref[...]), 0.0)
    acs = _cumsum_rows(dt * a_row)
    acs_t = acs.T
    row = lax.broadcasted_iota(jnp.int32, (q, q), 0)
    col = lax.broadcasted_iota(jnp.int32, (q, q), 1)
    causal = row >= col

    inner = SSM_HEADS * SSM_HEADDIM
    assert SSM_HEADDIM & (SSM_HEADDIM - 1) == 0
    head_of_lane = lax.broadcasted_iota(jnp.int32, (LANES, inner), 1) >> (SSM_HEADDIM.bit_length() - 1)
    replicate = jnp.where(head_of_lane == lax.broadcasted_iota(jnp.int32, (LANES, inner), 0), 1.0, 0.0).astype(BF16)
    acs_x = None
    for part in _split3(acs):
        term = _mm(part, replicate)
        acs_x = term if acs_x is None else acs_x + term
    tot_x = acs_x[q - 1:q, :]
    grow_x = jnp.exp(acs_x)
    dend_x = jnp.exp(tot_x - acs_x)
    dec_x = jnp.exp(tot_x)

    groups = []
    for g in range(SSM_GROUPS):
        bmat = bm_ref[rows, g * SSM_STATE:(g + 1) * SSM_STATE]
        cmat = cm_ref[rows, g * SSM_STATE:(g + 1) * SSM_STATE]
        cb = lax.dot_general(cmat, bmat, (((1,), (1,)), ((), ())), preferred_element_type=F32)
        prev_t = state[g]
        y_off = _mm(cmat, prev_t.astype(BF16))
        groups.append((bmat, cb, prev_t, y_off))

    for g, (bmat, cb, prev_t, y_off) in enumerate(groups):
        gcols = slice(g * gw, (g + 1) * gw)
        xd_parts, u_parts = [], []
        for p in range(pairs_per_group):
            c0 = g * gw + p * LANES
            h0 = c0 // SSM_HEADDIM
            pcols = slice(c0, c0 + LANES)
            x2 = xs_ref[rows, pcols]
            dt2 = jnp.where(first_head, dt[:, h0:h0 + 1], dt[:, h0 + 1:h0 + 2])
            xdt2 = x2 * dt2
            xdt2_b = xdt2.astype(BF16)
            ydiag = []
            for e in (0, 1):
                seg = jnp.exp(jnp.where(causal, acs[:, h0 + e:h0 + e + 1] - acs_t[h0 + e:h0 + e + 1, :], -jnp.inf))
                ydiag.append(_mm((cb * seg).astype(BF16), xdt2_b))
            y2 = jnp.where(first_head, ydiag[0], ydiag[1]) + y_off[:, p * LANES:(p + 1) * LANES] * grow_x[:, pcols]
            xd_parts.append((xdt2 * dend_x[:, pcols]).astype(BF16))
            y2 = y2 + dskip_ref[:, pcols] * x2
            zz = z_ref[rows, pcols]
            u_parts.append(y2 * (zz * _sigmoid(zz)))
        xd = jnp.concatenate(xd_parts, axis=1)
        new_t = lax.dot_general(bmat, xd, (((0,), (0,)), ((), ())), preferred_element_type=F32)
        state[g] = prev_t * dec_x[:, gcols] + new_t
        u = jnp.concatenate(u_parts, axis=1)
        ms = jnp.mean(u * u, axis=-1, keepdims=True)
        o_ref[rows, gcols] = (u * lax.rsqrt(ms + LN_EPS) * ng_ref[:, gcols]).astype(o_ref.dtype)


def _ssd_call(xs, bm, cm, dt, z, alog_row, dskip_row, ng_row, *, batch, seq):
    t, inner = xs.shape
    q = SSM_CHUNK * SSD_CHUNKS_PER_STEP
    nc = seq // q
    gw = inner // SSM_GROUPS
    rows = lambda b, c: (b * nc + c, 0)
    const = lambda b, c: (0, 0)
    return pl.pallas_call(
        _ssd_kernel,
        grid=(batch, nc),
        in_specs=[pl.BlockSpec((q, inner), rows),
                  pl.BlockSpec((q, bm.shape[1]), rows),
                  pl.BlockSpec((q, cm.shape[1]), rows),
                  pl.BlockSpec((q, LANES), rows),
                  pl.BlockSpec((q, inner), rows),
                  pl.BlockSpec((1, LANES), const),
                  pl.BlockSpec((1, inner), const),
                  pl.BlockSpec((1, inner), const)],
        out_specs=pl.BlockSpec((q, inner), rows),
        out_shape=jax.ShapeDtypeStruct((t, inner), BF16),
        scratch_shapes=[pltpu.VMEM((SSM_GROUPS, SSM_STATE, gw), F32)],
        compiler_params=_params(("arbitrary", "arbitrary")),
        name="ssd_scan",
    )(xs, bm, cm, dt, z, alog_row, dskip_row, ng_row)


def _split3(x):
    hi = x.astype(BF16)
    r1 = x - hi.astype(F32)
    mid = r1.astype(BF16)
    lo = (r1 - mid.astype(F32)).astype(BF16)
    return hi, mid, lo


def _l1_in_kernel(x_ref, w_ref, wf_ref, bf_ref, qkv_ref, eq_ref, ek_ref, carry, *, tiles_per_seq, q_width):
    seq_start = pl.program_id(0) % tiles_per_seq == 0
    items = []
    for cols in _chunks(qkv_ref.shape[1]):
        def store(y, cols=cols):
            if cols.stop <= q_width:
                y = y * (FOX_HEADDIM ** -0.5 * LOG2E)
            qkv_ref[:, cols] = y.astype(qkv_ref.dtype)
        items.append((lambda cols=cols: _mm(x_ref[...], w_ref[:, cols]), store))

    def gate_scan(f_raw):
        f = f_raw + bf_ref[...]
        cum = _cumsum_rows(-_softplus(-f)) + jnp.where(seq_start, 0.0, carry[...])
        carry[...] = cum[cum.shape[0] - 1:, :]
        parts = _split3(cum * LOG2E)
        r = lax.broadcasted_iota(jnp.int32, (LANES, LANES), 0)
        l = lax.broadcasted_iota(jnp.int32, (LANES, LANES), 1)
        lane = lax.broadcasted_iota(jnp.int32, (1, LANES), 1) & (FOX_EXT - 1)
        spread_q = spread_k = None
        for c, part in enumerate(parts):
            sq = _mm(part, jnp.where(l == FOX_EXT * r + c, 1.0, 0.0).astype(BF16))
            sk = _mm(part, jnp.where(l == FOX_EXT * r + 3 + c, 1.0, 0.0).astype(BF16))
            spread_q = sq if spread_q is None else spread_q + sq
            spread_k = sk if spread_k is None else spread_k + sk
        ones_q = jnp.where((lane >= 3) & (lane < 6), 1.0, 0.0)
        ones_k = jnp.where(lane < 3, 1.0, 0.0)
        eq_ref[...] = (spread_q + ones_q).astype(BF16)
        ek_ref[...] = (ones_k - spread_k).astype(BF16)

    _issue_ahead(items + [(lambda: _mm(x_ref[...], wf_ref[...]), gate_scan)])


def _l1_in_call(xb, w_qkv, w_f, b_f, *, seq):
    t, d = xb.shape
    n = w_qkv.shape[1]
    tm = PROJ_ROW_TILE
    return pl.pallas_call(
        functools.partial(_l1_in_kernel, tiles_per_seq=seq // tm, q_width=n // 3),
        grid=(t // tm,),
        in_specs=[_row_spec(d, tm), _resident(w_qkv.shape), _resident(w_f.shape), _resident(b_f.shape)],
        out_specs=[_row_spec(n, tm), _row_spec(LANES, tm), _row_spec(LANES, tm)],
        out_shape=[jax.ShapeDtypeStruct((t, n), BF16), jax.ShapeDtypeStruct((t, LANES), BF16),
                   jax.ShapeDtypeStruct((t, LANES), BF16)],
        scratch_shapes=[pltpu.VMEM((1, LANES), F32)],
        compiler_params=_params(("arbitrary",)),
        name="l1_in",
    )(xb, w_qkv, w_f, b_f)


def _fox_attn_kernel(q_ref, k_ref, v_ref, eq_ref, ek_ref, o_ref, vt_ref, *, tq, nq):
    for pp in range(FOX_PAIRS_PER_STEP):
        lanes = slice(pp * LANES, (pp + 1) * LANES)
        _fox_attn_pair(FOX_PAIRS_PER_STEP * pl.program_id(1) + pp, q_ref.at[:, lanes], k_ref.at[:, lanes],
                       v_ref.at[:, lanes], eq_ref, ek_ref, o_ref.at[:, lanes], vt_ref.at[pp], tq=tq, nq=nq)


def _fox_attn_pair(hp, q_ref, k_ref, v_ref, eq_ref, ek_ref, o_ref, vt_ref, *, tq, nq):
    lane2 = lax.broadcasted_iota(jnp.int32, (1, 2 * LANES), 1)
    head_lanes = []
    for e in (0, 1):
        ext0 = LANES + FOX_EXT * (2 * hp + e)
        head_lanes.append(((lane2 >= FOX_HEADDIM * e) & (lane2 < FOX_HEADDIM * (e + 1)))
                          | ((lane2 >= ext0) & (lane2 < ext0 + FOX_EXT)))
    key = lax.broadcasted_iota(jnp.int32, (tq, tq), 0)
    qry = lax.broadcasted_iota(jnp.int32, (tq, tq), 1)
    visible = key <= qry

    vrows = FOX_HEADDIM + FOX_ONES
    for j in range(nq):
        v_t = v_ref[j * tq:(j + 1) * tq, :].astype(F32).T.astype(BF16)
        for e in (0, 1):
            vt_ref[j, e, 0:FOX_HEADDIM, :] = v_t[FOX_HEADDIM * e:FOX_HEADDIM * (e + 1), :]
            vt_ref[j, e, FOX_HEADDIM:vrows, :] = jnp.ones((FOX_ONES, tq), BF16)

    def head_queries(i):
        rows = slice(i * tq, (i + 1) * tq)
        qe = jnp.concatenate([q_ref[rows, :], eq_ref[rows, :]], axis=1)
        return [jnp.where(m, qe, jnp.zeros_like(qe)) for m in head_lanes]

    def logits(qm, j):
        rows = slice(j * tq, (j + 1) * tq)
        ke = jnp.concatenate([k_ref[rows, :], ek_ref[rows, :]], axis=1)
        return lax.dot_general(ke, qm, (((1,), (1,)), ((), ())), preferred_element_type=F32)

    def softmax(s, m, diagonal):
        if diagonal:
            s = jnp.where(visible, s, NEG_BIG)
        m_new = jnp.maximum(m, jnp.max(s, axis=0, keepdims=True))
        return m_new, jnp.exp2(m - m_new), jnp.exp2(s - m_new).astype(BF16)

    steps = [(i, j) for i in range(nq) for j in range(i + 1)]
    queries = {}

    def masked_queries(i):
        if i not in queries:
            queries[i] = head_queries(i)
        return queries[i]

    outs = {}

    def finish(i, e, acc):
        outs.setdefault(i, {})[e] = acc[0:FOX_HEADDIM] / acc[FOX_HEADDIM:FOX_HEADDIM + 1]
        if len(outs[i]) == 2:
            o_t = jnp.concatenate([outs[i][0], outs[i][1]], axis=0)
            o_ref[i * tq:(i + 1) * tq, :] = o_t.T.astype(o_ref.dtype)
            del outs[i]

    class Stream:
        def __init__(self, e, steps):
            self.e = e
            self.steps = steps
            self.s_next = logits(masked_queries(steps[0][0])[e], steps[0][1])
            self.pending = None
            self.acc = {}
            self.m = None

        def retire(self, pending, pv):
            pi, pj, _, alpha = pending
            self.acc[pi] = pv if pj == 0 else alpha * self.acc[pi] + pv
            if pj == pi:
                finish(pi, self.e, self.acc.pop(pi))

        def step(self, k):
            if k >= len(self.steps):
                return
            i, j = self.steps[k]
            s = self.s_next
            if k + 1 < len(self.steps):
                self.s_next = logits(masked_queries(self.steps[k + 1][0])[self.e], self.steps[k + 1][1])
            pv = None if self.pending is None else _mm(vt_ref[self.pending[1], self.e], self.pending[2])
            if j == 0:
                self.m = jnp.full((1, tq), NEG_BIG, F32)
            self.m, alpha, p = softmax(s, self.m, j == i)
            if self.pending is not None:
                self.retire(self.pending, pv)
            self.pending = (i, j, p, alpha)

        def drain(self):
            self.retire(self.pending, _mm(vt_ref[self.pending[1], self.e], self.pending[2]))

    group = lambda i: (i % 4) in (0, 3)
    assert nq % 4 == 0
    lo = [st for st in steps if group(st[0])]
    hi = [st for st in steps if not group(st[0])]
    streams = [Stream(0, lo), Stream(1, lo), Stream(0, hi), Stream(1, hi)]
    for k in range(max(len(stream.steps) for stream in streams)):
        for stream in streams:
            stream.step(k)
    for stream in streams:
        stream.drain()


def _fox_attn_call(qkv, eq, ek, *, batch, seq, tq):
    t = qkv.shape[0]
    width = FOX_HEADS * FOX_HEADDIM
    n_steps = width // LANES // FOX_PAIRS_PER_STEP
    assert n_steps * FOX_PAIRS_PER_STEP * LANES == width
    slab = FOX_PAIRS_PER_STEP * LANES
    blk = lambda section: pl.BlockSpec((seq, slab), lambda b, g, s=section: (b, s * n_steps + g))
    ext = pl.BlockSpec((seq, LANES), lambda b, g: (b, 0))
    return pl.pallas_call(
        functools.partial(_fox_attn_kernel, tq=tq, nq=seq // tq),
        grid=(batch, n_steps),
        in_specs=[blk(0), blk(1), blk(2), ext, ext],
        out_specs=pl.BlockSpec((seq, slab), lambda b, g: (b, g)),
        out_shape=jax.ShapeDtypeStruct((t, width), BF16),
        scratch_shapes=[pltpu.VMEM((FOX_PAIRS_PER_STEP, seq // tq, 2, FOX_HEADDIM + FOX_ONES, tq), BF16)],
        compiler_params=_params(("arbitrary", "arbitrary")),
        name="fox_attn",
    )(qkv, qkv, qkv, eq, ek)


def _pad_cols(a, width):
    return jnp.pad(a, ((0, 0), (0, width - a.shape[1])))


def kernel(x, sc_ssm_w_in, sc_conv_w, ssm_conv_w, ssm_conv_b, ssm_dt_bias, ssm_a_log, ssm_d, ssm_norm_g,
           sc_ssm_w_out, fox_w_in, fox_b_f, fox_w_out, ffn_w_up, ffn_conv_w, ffn_conv_b, ffn_w_down,
           ln_mix_g, ln_mix_b, ln_ffn_g, ln_ffn_b):
    batch, seq, d = x.shape
    t = batch * seq
    x32 = x.reshape(t, d)
    w_up, w_down = ffn_w_up.astype(BF16), ffn_w_down.astype(BF16)

    inner = SSM_HEADS * SSM_HEADDIM
    bc_w = SSM_GROUPS * SSM_STATE
    o_dt = 3 * d + 2 * inner + 2 * bc_w
    ya, z, xs, bm, cm, dt = _l0_in_call(
        x32, sc_ssm_w_in[0].astype(BF16), _pad_cols(sc_ssm_w_in[0][:, o_dt:], LANES).astype(BF16),
        sc_conv_w[0], ssm_conv_w[0], ssm_conv_b[0].reshape(1, -1), _pad_cols(ssm_dt_bias[0].reshape(1, -1), LANES),
        seq=seq, inner=inner, bc_w=bc_w)
    yb = _ssd_call(xs, bm, cm, dt, z,
                   _pad_cols(ssm_a_log[0].reshape(1, -1), LANES),
                   jnp.repeat(ssm_d[0], SSM_HEADDIM).reshape(1, -1),
                   ssm_norm_g[0].reshape(1, -1), batch=batch, seq=seq)
    x32 = _outproj_ln([ya, yb], sc_ssm_w_out[0].astype(BF16), x32, ln_mix_g[0], ln_mix_b[0], name="l0_out")
    x32, xb = _conv_ffn(x32, w_up, ffn_conv_w[0], ffn_conv_b[0], w_down, ln_ffn_g[0], ln_ffn_b[0],
                        layer=0, seq=seq, name="l0_ffn", emit_bf16=True)

    width = FOX_HEADS * FOX_HEADDIM
    fw = fox_w_in[0]
    qkv, eq, ek = _l1_in_call(xb, fw[:, :3 * width].astype(BF16), _pad_cols(fw[:, 3 * width:], LANES).astype(BF16),
                              _pad_cols(fox_b_f[0].reshape(1, -1), LANES), seq=seq)
    o = _fox_attn_call(qkv, eq, ek, batch=batch, seq=seq, tq=256)
    x32 = _outproj_ln([o], fox_w_out[0].astype(BF16), x32, ln_mix_g[1], ln_mix_b[1], name="l1_out")
    (x32,) = _conv_ffn(x32, w_up, ffn_conv_w[1], ffn_conv_b[1], w_down, ln_ffn_g[1], ln_ffn_b[1],
                       layer=1, seq=seq, name="l1_ffn", emit_bf16=False)
    return x32.reshape(batch, seq, d)
```
